```python
import math
import jax, jax.numpy as jnp
from jax import lax
import numpy as np

D_MODEL = 1024
BATCH = 16
SEQ = 2048
DEPTH = 2

HEAD_DIM = 64
CONV_CH = 256
CONV_WIDTH = 31
DSA_HEADS = 6
MOBA_HEADS = 6
DSA_W = DSA_HEADS * HEAD_DIM
MOBA_W = MOBA_HEADS * HEAD_DIM
MIX_W = CONV_CH + DSA_W + MOBA_W
IDX_HEADS = 4
IDX_DIM = 32
DSA_TOPK = 256
DSA_Q_BLOCK = 128
MOBA_BLOCK = 256
MOBA_TOPK = 3
MOBA_Q_BLOCK = 16
ROPE_THETA = 500000.0
ROPE_FRACTION = 4
N_EXPERTS = 16
N_EXPERT_GROUPS = 4
EXPERTS_PER_GROUP = N_EXPERTS // N_EXPERT_GROUPS
TOP_K = 2
D_EXPERT = 256
EPS = 1e-6
IN_SPLITS = (CONV_CH, CONV_CH, DSA_W, HEAD_DIM, HEAD_DIM, IDX_HEADS * IDX_DIM, IDX_DIM, IDX_HEADS, MOBA_W, MOBA_W, MOBA_W)
IN_W = 2 * CONV_CH + DSA_W + 2 * HEAD_DIM + IDX_HEADS * IDX_DIM + IDX_DIM + IDX_HEADS + 3 * MOBA_W

kernel_name = 'hybrid_conv_dsa_moba_moe_block'

F32 = jnp.float32


def rms_norm(x, g):
    xf = x.astype(F32)
    y = xf * lax.rsqrt(jnp.mean(xf * xf, axis=-1, keepdims=True) + EPS)
    return (y * g.astype(F32)).astype(x.dtype)


def modulate(h, shift, scale):
    return h * (1 + scale[:, None, :]) + shift[:, None, :]


def rope_partial(x, pos):
    d = x.shape[-1]
    rot = d // ROPE_FRACTION
    half = rot // 2
    inv = jnp.exp(-math.log(ROPE_THETA) * (jnp.arange(half, dtype=F32) * 2.0 / rot))
    ang = pos.astype(F32)[:, None] * inv[None, :]
    cos = jnp.cos(ang)[:, None, :]
    sin = jnp.sin(ang)[:, None, :]
    xf = x.astype(F32)
    x1 = xf[..., :half]
    x2 = xf[..., half:rot]
    out = jnp.concatenate([x1 * cos - x2 * sin, x2 * cos + x1 * sin, xf[..., rot:]], axis=-1)
    return out.astype(x.dtype)


def conformer_conv(u, g, w_dw, b_dw, ln_g, ln_b):
    a = u * jax.nn.sigmoid(g)
    y = lax.conv_general_dilated(a, w_dw[:, None, :].astype(a.dtype), window_strides=(1,),
                                 padding=((CONV_WIDTH - 1, 0),),
                                 dimension_numbers=('NWC', 'WIO', 'NWC'),
                                 feature_group_count=CONV_CH)
    y = (y + b_dw).astype(F32)
    mu = jnp.mean(y, axis=-1, keepdims=True)
    var = jnp.mean(jnp.square(y - mu), axis=-1, keepdims=True)
    y = (y - mu) * lax.rsqrt(var + EPS) * ln_g.astype(F32) + ln_b.astype(F32)
    return jax.nn.silu(y).astype(u.dtype)


def dsa_attention(q, k, v, qi, ki, wi):
    B, L, H, dh = q.shape
    topk = min(DSA_TOPK, L // 4)
    nq = L // DSA_Q_BLOCK
    key_pos = jnp.arange(L)
    ki32 = ki.astype(F32)
    scale = dh ** -0.5

    def blocks(a):
        return jnp.moveaxis(a.reshape((B, nq, DSA_Q_BLOCK) + a.shape[2:]), 1, 0)

    def one_block(args):
        qb, qib, wib, t0 = args
        tq = t0 + jnp.arange(DSA_Q_BLOCK)
        logits = jnp.einsum('bqhd,bsd->bqhs', qib.astype(F32), ki32) * (IDX_DIM ** -0.5)
        score = jnp.einsum('bqh,bqhs->bqs', wib.astype(F32) * (IDX_HEADS ** -0.5), jax.nn.relu(logits))
        causal = key_pos[None, :] <= tq[:, None]
        score = jnp.where(causal[None], score, -jnp.inf)
        _, idx = lax.top_k(score, topk)
        valid = idx <= tq[None, :, None]
        k_sel = jax.vmap(lambda kk, ii: kk[ii])(k, idx)
        v_sel = jax.vmap(lambda vv, ii: vv[ii])(v, idx)
        s = jnp.einsum('bqhd,bqkd->bqhk', qb, k_sel).astype(F32) * scale
        s = jnp.where(valid[:, :, None, :], s, -jnp.inf)
        p = jax.nn.softmax(s, axis=-1).astype(v.dtype)
        return jnp.einsum('bqhk,bqkd->bqhd', p, v_sel)

    t0s = jnp.arange(nq, dtype=jnp.int32) * DSA_Q_BLOCK
    out = lax.map(one_block, (blocks(q), blocks(qi), blocks(wi), t0s))
    return jnp.moveaxis(out, 0, 1).reshape(B, L, H * dh)


def moba_attention(q, k, v):
    B, L, H, dh = q.shape
    nb = -(-L // MOBA_BLOCK)
    n_sel = min(MOBA_TOPK, nb - 1)
    pad = nb * MOBA_BLOCK - L
    scale = dh ** -0.5

    def to_blocks(a):
        a = jnp.pad(a, ((0, 0), (0, pad), (0, 0), (0, 0)))
        return a.reshape(B, nb, MOBA_BLOCK, H, dh).transpose(0, 3, 1, 2, 4)

    kb = to_blocks(k)
    vb = to_blocks(v)
    k_mean = jnp.mean(kb.astype(F32), axis=3)
    nq = L // MOBA_Q_BLOCK
    b_i = jnp.arange(B)[:, None, None, None]
    h_i = jnp.arange(H)[None, None, :, None]
    blk_ids = jnp.arange(nb)
    in_blk = jnp.arange(MOBA_BLOCK)

    def one_block(args):
        qb, t0 = args
        tq = t0 + jnp.arange(MOBA_Q_BLOCK)
        own = t0 // MOBA_BLOCK
        k_own = lax.dynamic_index_in_dim(kb, own, axis=2, keepdims=False)
        v_own = lax.dynamic_index_in_dim(vb, own, axis=2, keepdims=False)
        own_mask = (own * MOBA_BLOCK + in_blk)[None, :] <= tq[:, None]
        s_own = jnp.einsum('bqhd,bhkd->bqhk', qb, k_own).astype(F32) * scale
        s_own = jnp.where(own_mask[None, :, None, :], s_own, -jnp.inf)
        if n_sel == 0:
            p_own = jax.nn.softmax(s_own, axis=-1).astype(v.dtype)
            return jnp.einsum('bqhk,bhkd->bqhd', p_own, v_own)
        gate = jnp.einsum('bqhd,bhnd->bqhn', qb.astype(F32), k_mean)
        gate = jnp.where(blk_ids < own, gate, -jnp.inf)
        _, sel = lax.top_k(gate, n_sel)
        valid = sel < own
        k_sel = kb[b_i, h_i, sel]
        v_sel = vb[b_i, h_i, sel]
        s_past = jnp.einsum('bqhd,bqhnkd->bqhnk', qb, k_sel).astype(F32) * scale
        s_past = jnp.where(valid[..., None], s_past, -jnp.inf)
        s = jnp.concatenate([s_past.reshape(B, MOBA_Q_BLOCK, H, n_sel * MOBA_BLOCK), s_own], axis=-1)
        p = jax.nn.softmax(s, axis=-1).astype(v.dtype)
        p_past = p[..., :n_sel * MOBA_BLOCK].reshape(B, MOBA_Q_BLOCK, H, n_sel, MOBA_BLOCK)
        p_own = p[..., n_sel * MOBA_BLOCK:]
        return (jnp.einsum('bqhnk,bqhnkd->bqhd', p_past, v_sel)
                + jnp.einsum('bqhk,bhkd->bqhd', p_own, v_own))

    q_blocks = jnp.moveaxis(q.reshape(B, nq, MOBA_Q_BLOCK, H, dh), 1, 0)
    t0s = jnp.arange(nq, dtype=jnp.int32) * MOBA_Q_BLOCK
    out = lax.map(one_block, (q_blocks, t0s))
    return jnp.moveaxis(out, 0, 1).reshape(B, L, H * dh)


def token_mixers(h, pos, w_in, w_dw, b_dw, ln_g, ln_b, w_o):
    B, L, _ = h.shape
    proj = h @ w_in
    offs = np.cumsum(IN_SPLITS)[:-1].tolist()
    cu, cg, dq, dk, dv, iq, ik, iw, mq, mk, mv = jnp.split(proj, offs, axis=-1)
    y_conv = conformer_conv(cu, cg, w_dw, b_dw, ln_g, ln_b)
    dq = rope_partial(dq.reshape(B, L, DSA_HEADS, HEAD_DIM), pos)
    dk = rope_partial(dk[:, :, None, :], pos)[:, :, 0, :]
    iq = rope_partial(iq.reshape(B, L, IDX_HEADS, IDX_DIM), pos)
    ik = rope_partial(ik[:, :, None, :], pos)[:, :, 0, :]
    y_dsa = dsa_attention(dq, dk, dv, iq, ik, iw)
    mq = rope_partial(mq.reshape(B, L, MOBA_HEADS, HEAD_DIM), pos)
    mk = rope_partial(mk.reshape(B, L, MOBA_HEADS, HEAD_DIM), pos)
    mv = mv.reshape(B, L, MOBA_HEADS, HEAD_DIM)
    y_moba = moba_attention(mq, mk, mv)
    return jnp.concatenate([y_conv, y_dsa, y_moba], axis=-1) @ w_o


def grouped_moe(h, w_router, b_router, w_gate, w_up, w_down):
    B, L, D = h.shape
    aff = jax.nn.sigmoid(jnp.einsum('bld,de->ble', h.astype(F32), w_router.astype(F32)))
    biased = aff + b_router.astype(F32)
    grp = biased.reshape(B, L, N_EXPERT_GROUPS, EXPERTS_PER_GROUP)
    grp_score = jnp.sum(lax.top_k(grp, TOP_K)[0], axis=-1)
    best = jnp.argmax(grp_score, axis=-1)
    in_grp = (jnp.arange(N_EXPERTS) // EXPERTS_PER_GROUP) == best[..., None]
    _, top_idx = lax.top_k(jnp.where(in_grp, biased, -jnp.inf), TOP_K)
    w = jnp.take_along_axis(aff, top_idx, axis=-1)
    w = w / jnp.sum(w, axis=-1, keepdims=True)
    gates = jnp.sum(jax.nn.one_hot(top_idx, N_EXPERTS, dtype=F32) * w[..., None], axis=-2).astype(h.dtype)
    out = jnp.zeros_like(h)
    for e in range(N_EXPERTS):
        a = jax.nn.silu(h @ w_gate[e]) * (h @ w_up[e])
        out = out + gates[..., e:e + 1] * (a @ w_down[e])
    return out


def setup_inputs(seed: int = 0) -> dict:
    key = jax.random.key(seed)
    ks = jax.random.split(key, 18)

    def nrm(k, shape, s):
        return jax.random.normal(k, shape, F32) * s

    return {
        'x': nrm(ks[0], (BATCH, SEQ, D_MODEL), 1.0),
        'c': nrm(ks[1], (BATCH, D_MODEL), 1.0),
        'w_ada': nrm(ks[2], (DEPTH, D_MODEL, 6 * D_MODEL), 0.5 * D_MODEL ** -0.5),
        'b_ada': nrm(ks[3], (DEPTH, 6 * D_MODEL), 0.02),
        'g_mix': 1.0 + nrm(ks[4], (DEPTH, D_MODEL), 0.02),
        'w_in': nrm(ks[5], (DEPTH, D_MODEL, IN_W), D_MODEL ** -0.5),
        'w_dw': nrm(ks[6], (DEPTH, CONV_WIDTH, CONV_CH), CONV_WIDTH ** -0.5),
        'b_dw': nrm(ks[7], (DEPTH, CONV_CH), 0.02),
        'ln_conv_g': 1.0 + nrm(ks[8], (DEPTH, CONV_CH), 0.02),
        'ln_conv_b': nrm(ks[9], (DEPTH, CONV_CH), 0.02),
        'w_o': nrm(ks[10], (DEPTH, MIX_W, D_MODEL), MIX_W ** -0.5),
        'g_ffn': 1.0 + nrm(ks[11], (DEPTH, D_MODEL), 0.02),
        'w_router': nrm(ks[12], (D_MODEL, N_EXPERTS), D_MODEL ** -0.5),
        'b_router': nrm(ks[13], (N_EXPERTS,), 0.01),
        'w_gate': nrm(ks[14], (DEPTH, N_EXPERTS, D_MODEL, D_EXPERT), D_MODEL ** -0.5),
        'w_up': nrm(ks[15], (DEPTH, N_EXPERTS, D_MODEL, D_EXPERT), D_MODEL ** -0.5),
        'w_down': nrm(ks[16], (DEPTH, N_EXPERTS, D_EXPERT, D_MODEL), D_EXPERT ** -0.5),
        'g_final': 1.0 + nrm(ks[17], (D_MODEL,), 0.02),
    }


def reference(x, c, w_ada, b_ada, g_mix, w_in, w_dw, b_dw, ln_conv_g, ln_conv_b, w_o, g_ffn,
              w_router, b_router, w_gate, w_up, w_down, g_final):
    L = x.shape[1]
    pos = jnp.arange(L)
    c_act = jax.nn.silu(c)
    for l in range(DEPTH):
        mod = c_act @ w_ada[l] + b_ada[l]
        sh1, sc1, g1, sh2, sc2, g2 = jnp.split(mod, 6, axis=-1)
        h = modulate(rms_norm(x, g_mix[l]), sh1, sc1)
        x = x + g1[:, None, :] * token_mixers(h, pos, w_in[l], w_dw[l], b_dw[l], ln_conv_g[l], ln_conv_b[l], w_o[l])
        h = modulate(rms_norm(x, g_ffn[l]), sh2, sc2)
        x = x + g2[:, None, :] * grouped_moe(h, w_router, b_router, w_gate[l], w_up[l], w_down[l])
    return rms_norm(x, g_final)
```

```python
import functools
import math

import jax
import jax.numpy as jnp
from jax import lax
from jax.experimental import pallas as pl
from jax.experimental.pallas import tpu as pltpu

F32 = jnp.float32
BF16 = jnp.bfloat16

D_MODEL = 1024
HEAD_DIM = 64
CONV_CH = 256
CONV_WIDTH = 31
DSA_HEADS = 6
MOBA_HEADS = 6
DSA_W = DSA_HEADS * HEAD_DIM
MOBA_W = MOBA_HEADS * HEAD_DIM
MIX_W = CONV_CH + DSA_W + MOBA_W
IDX_HEADS = 4
IDX_DIM = 32
DSA_TOPK = 256
DSA_Q_BLOCK = 128
MOBA_BLOCK = 256
MOBA_TOPK = 3
ROPE_THETA = 500000.0
ROPE_FRACTION = 4
N_EXPERTS = 16
N_EXPERT_GROUPS = 4
EXPERTS_PER_GROUP = N_EXPERTS // N_EXPERT_GROUPS
TOP_K = 2
D_EXPERT = 256
EPS = 1e-6

LANES = 128
NEG = -1e30
VMEM_LIMIT = 56 * 1024 * 1024

_OFF = {}
_o = 0
for _name, _w in (("cu", CONV_CH), ("cg", CONV_CH), ("dq", DSA_W), ("dk", HEAD_DIM), ("dv", HEAD_DIM),
                  ("iq", IDX_HEADS * IDX_DIM), ("ik", IDX_DIM), ("iw", IDX_HEADS),
                  ("mq", MOBA_W), ("mk", MOBA_W), ("mv", MOBA_W)):
    _OFF[_name] = (_o, _o + _w)
    _o += _w

C_CUCG = (0, 512)
C_DQ = (512, 896)
C_IQ = (896, 1024)
C_KK = (1024, 1152)
C_MQ = (1152, 1536)
C_MK = (1536, 1920)
N_PROJ = 1920
N_PROJ_T = 464

NT_DIMS = (((1,), (1,)), ((), ()))


def _nt(a, b, precision=None):
    return lax.dot_general(a, b, NT_DIMS, preferred_element_type=F32, precision=precision)


def _cparams(*sem):
    return pltpu.CompilerParams(dimension_semantics=sem, vmem_limit_bytes=VMEM_LIMIT)


def _ada_kernel(c_ref, w_ref, b_ref, o_ref):
    c = c_ref[...]
    ca = c * jax.nn.sigmoid(c)
    o_ref[...] = jnp.dot(ca, w_ref[...], preferred_element_type=F32,
                         precision=lax.Precision.HIGHEST) + b_ref[...]


def _ada(c, w_ada, b_ada):
    depth, d, n = w_ada.shape
    bsz = c.shape[0]
    tn = 1536
    return pl.pallas_call(
        _ada_kernel,
        grid=(depth, n // tn),
        in_specs=[pl.BlockSpec((bsz, d), lambda l, j: (0, 0)),
                  pl.BlockSpec((None, d, tn), lambda l, j: (l, 0, j)),
                  pl.BlockSpec((None, 1, tn), lambda l, j: (l, 0, j))],
        out_specs=pl.BlockSpec((None, bsz, tn), lambda l, j: (l, 0, j)),
        out_shape=jax.ShapeDtypeStruct((depth, bsz, n), F32),
        name="ada_params",
        compiler_params=_cparams("arbitrary", "arbitrary"),
    )(c, w_ada, b_ada.reshape(depth, 1, n))


def _rms_mod(x, g, shift, scale):
    ms = jnp.mean(x * x, axis=-1, keepdims=True)
    h = x * lax.rsqrt(ms + EPS) * g
    return h * (1.0 + scale) + shift


def _in_kernel(x_ref, mod_ref, g_ref, w_ref, wt_ref, rope_ref,
               cucg_ref, dq_ref, iq_ref, kk_ref, mq_ref, mk_ref, kmean_ref, vt_ref, mvt_ref, iwt_ref):
    tm = x_ref.shape[0]
    h = _rms_mod(x_ref[...], g_ref[...], mod_ref[0:1, :], mod_ref[1:2, :])
    hb = h.astype(BF16)

    def proj(cols):
        return jnp.dot(hb, w_ref[:, cols[0]:cols[1]], preferred_element_type=F32)

    def rope(v, t0, half):
        return (v * rope_ref[t0] + pltpu.roll(v, LANES - half, 1) * rope_ref[t0 + 1]
                + pltpu.roll(v, half, 1) * rope_ref[t0 + 2])

    cucg_ref[...] = proj(C_CUCG).astype(BF16)

    p = proj(C_DQ)
    for ch in range(3):
        sl = slice(ch * LANES, (ch + 1) * LANES)
        dq_ref[:, sl] = rope(p[:, sl], 0, 8).astype(BF16)

    iq_ref[...] = rope(proj(C_IQ), 3, 4).astype(BF16)

    p = proj(C_KK)
    kk = (p * rope_ref[6] + pltpu.roll(p, LANES - 8, 1) * rope_ref[7] + pltpu.roll(p, 8, 1) * rope_ref[8]
          + pltpu.roll(p, LANES - 4, 1) * rope_ref[9] + pltpu.roll(p, 4, 1) * rope_ref[10])
    kk_ref[...] = kk.astype(BF16)

    p = proj(C_MQ)
    for ch in range(3):
        sl = slice(ch * LANES, (ch + 1) * LANES)
        mq_ref[:, sl] = rope(p[:, sl], 0, 8).astype(BF16)

    p = proj(C_MK)
    for ch in range(3):
        sl = slice(ch * LANES, (ch + 1) * LANES)
        r = rope(p[:, sl], 0, 8)
        mk_ref[:, sl] = r.astype(BF16)
        for blk in range(tm // MOBA_BLOCK):
            kmean_ref[blk:blk + 1, sl] = jnp.mean(r[blk * MOBA_BLOCK:(blk + 1) * MOBA_BLOCK], axis=0, keepdims=True)

    pt = _nt(wt_ref[...], hb)
    vt_ref[...] = pt[0:HEAD_DIM].astype(BF16)
    mvt_ref[...] = pt[HEAD_DIM:HEAD_DIM + MOBA_W].astype(BF16)
    iwt_ref[...] = pt[HEAD_DIM + MOBA_W:N_PROJ_T]


def _in_proj(x, mod3, g, w_p, w_t, rope_tabs, tm=512):
    bsz, seq, d = x.shape
    nt = seq // tm
    nb = tm // MOBA_BLOCK
    tok = lambda w: pl.BlockSpec((None, tm, w), lambda b, t: (b, t, 0))
    out_shape = (
        jax.ShapeDtypeStruct((bsz, seq, 512), BF16),
        jax.ShapeDtypeStruct((bsz, seq, DSA_W), BF16),
        jax.ShapeDtypeStruct((bsz, seq, LANES), BF16),
        jax.ShapeDtypeStruct((bsz, seq, LANES), BF16),
        jax.ShapeDtypeStruct((bsz, seq, MOBA_W), BF16),
        jax.ShapeDtypeStruct((bsz, seq, MOBA_W), BF16),
        jax.ShapeDtypeStruct((bsz, nt, nb, MOBA_W), F32),
        jax.ShapeDtypeStruct((bsz, HEAD_DIM, seq), BF16),
        jax.ShapeDtypeStruct((bsz, MOBA_W, seq), BF16),
        jax.ShapeDtypeStruct((bsz, 16, seq), F32),
    )
    out_specs = (
        tok(512), tok(DSA_W), tok(LANES), tok(LANES), tok(MOBA_W), tok(MOBA_W),
        pl.BlockSpec((None, None, nb, MOBA_W), lambda b, t: (b, t, 0, 0)),
        pl.BlockSpec((None, HEAD_DIM, tm), lambda b, t: (b, 0, t)),
        pl.BlockSpec((None, MOBA_W, tm), lambda b, t: (b, 0, t)),
        pl.BlockSpec((None, 16, tm), lambda b, t: (b, 0, t)),
    )
    return pl.pallas_call(
        _in_kernel,
        grid=(bsz, nt),
        in_specs=[tok(d),
                  pl.BlockSpec((None, 6, d), lambda b, t: (b, 0, 0)),
                  pl.BlockSpec((1, d), lambda b, t: (0, 0)),
                  pl.BlockSpec((d, N_PROJ), lambda b, t: (0, 0)),
                  pl.BlockSpec((N_PROJ_T, d), lambda b, t: (0, 0)),
                  pl.BlockSpec((11, tm, LANES), lambda b, t: (0, t, 0))],
        out_specs=out_specs,
        out_shape=out_shape,
        name="in_proj",
        compiler_params=_cparams("parallel", "parallel"),
    )(x, mod3, g, w_p, w_t, rope_tabs)


CONV_PAD = 32
CONV_ROWS = 128


def _conv_kernel(cucg_ref, wdw_ref, bdw_ref, lg_ref, lb_ref, y_ref, a_ref):
    seq = cucg_ref.shape[0]
    a_ref[0:CONV_PAD, :] = jnp.zeros((CONV_PAD, CONV_CH), F32)

    def glu(c, carry):
        base = pl.multiple_of(c * 256, 256)
        u = cucg_ref[pl.ds(base, 256), 0:CONV_CH].astype(F32)
        g = cucg_ref[pl.ds(base, 256), CONV_CH:2 * CONV_CH].astype(F32)
        a_ref[pl.ds(base + CONV_PAD, 256), :] = u * jax.nn.sigmoid(g)
        return carry

    lax.fori_loop(0, seq // 256, glu, 0)

    def step(c, carry):
        base = pl.multiple_of(c * CONV_ROWS, CONV_ROWS)
        acc = jnp.zeros((CONV_ROWS, CONV_CH), F32) + bdw_ref[...]
        win = a_ref[pl.ds(base, CONV_ROWS + CONV_PAD), :]
        first = CONV_PAD - (CONV_WIDTH - 1)
        for r in range(8):
            offs = [o for o in range(first, first + CONV_WIDTH) if o % 8 == r]
            shifted = win[r:offs[-1] + CONV_ROWS, :]
            for o in offs:
                acc = acc + shifted[o - r:o - r + CONV_ROWS, :] * wdw_ref[o - first:o - first + 1, :]
        mu = jnp.mean(acc, axis=-1, keepdims=True)
        yc = acc - mu
        var = jnp.mean(yc * yc, axis=-1, keepdims=True)
        yn = yc * lax.rsqrt(var + EPS) * lg_ref[...] + lb_ref[...]
        y_ref[pl.ds(base, CONV_ROWS), :] = (yn * jax.nn.sigmoid(yn)).astype(BF16)
        return carry

    lax.fori_loop(0, seq // CONV_ROWS, step, 0)


def _conv(cucg, w_dw, b_dw, ln_g, ln_b):
    bsz, seq, _ = cucg.shape
    row = pl.BlockSpec((1, CONV_CH), lambda b: (0, 0))
    return pl.pallas_call(
        _conv_kernel,
        grid=(bsz,),
        in_specs=[pl.BlockSpec((None, seq, 2 * CONV_CH), lambda b: (b, 0, 0)),
                  pl.BlockSpec((CONV_WIDTH, CONV_CH), lambda b: (0, 0)),
                  row, row, row],
        out_specs=pl.BlockSpec((None, seq, CONV_CH), lambda b: (b, 0, 0)),
        out_shape=jax.ShapeDtypeStruct((bsz, seq, CONV_CH), BF16),
        scratch_shapes=[pltpu.VMEM((seq + CONV_PAD, CONV_CH), F32)],
        name="conv_group",
        compiler_params=_cparams("parallel"),
    )(cucg, w_dw, b_dw.reshape(1, -1), ln_g.reshape(1, -1), ln_b.reshape(1, -1))


DSA_KC = 256
INT_MIN = -2 ** 31


def _key_to_float(k):
    b = jnp.where(k < 0, jnp.int32(INT_MIN) - k, k)
    return lax.bitcast_convert_type(b, F32)


def _dsa_kernel(iq_ref, kk_ref, dq_ref, vt_ref, iwt_ref, y_ref, sc_ref, m_ref, l_ref, acc_ref):
    i = pl.program_id(1)
    nch = (i + 2) // 2
    qpos = i * DSA_Q_BLOCK + lax.broadcasted_iota(jnp.int32, (1, DSA_Q_BLOCK), 1)
    krow = lax.broadcasted_iota(jnp.int32, (DSA_KC, DSA_Q_BLOCK), 0)

    iq = iq_ref[...]
    iqh = [iq[:, h * IDX_DIM:(h + 1) * IDX_DIM] for h in range(IDX_HEADS)]
    wrow = [iwt_ref[h:h + 1, :] * (IDX_HEADS ** -0.5) for h in range(IDX_HEADS)]

    def idx_body(c, carry):
        base = pl.multiple_of(c * DSA_KC, DSA_KC)
        ikc = kk_ref[pl.ds(base, DSA_KC), HEAD_DIM:HEAD_DIM + IDX_DIM]
        sc = jnp.zeros((DSA_KC, DSA_Q_BLOCK), F32)
        for h in range(IDX_HEADS):
            lg = _nt(ikc, iqh[h]) * (IDX_DIM ** -0.5)
            sc = sc + jnp.maximum(lg, 0.0) * wrow[h]
        sc_ref[pl.ds(base, DSA_KC), :] = jnp.where(krow + base <= qpos, sc, -jnp.inf)
        return carry

    lax.fori_loop(0, nch, idx_body, 0)

    def count(pred):
        def body(c, acc):
            base = pl.multiple_of(c * DSA_KC, DSA_KC)
            hit = pred(sc_ref[pl.ds(base, DSA_KC), :], krow + base)
            return acc + jnp.sum(hit, axis=0, keepdims=True)
        return lax.fori_loop(0, nch, body, jnp.zeros((1, DSA_Q_BLOCK), F32))

    topk = float(DSA_TOPK)
    cnt = count(lambda s, k: jnp.where(s >= 0.0, 1.0, 0.0))
    cand0 = jnp.where(cnt >= topk, jnp.int32(0), jnp.int32(INT_MIN))

    def bit_body(step, cand):
        t = cand | (jnp.int32(1) << (30 - step))
        tf = _key_to_float(t)
        cnt = count(lambda s, k: jnp.where(s >= tf, 1.0, 0.0))
        return jnp.where(cnt >= topk, t, cand)

    cand = lax.fori_loop(0, 31, bit_body, cand0)
    thr = jnp.where(cand == jnp.int32(INT_MIN), -jnp.inf, _key_to_float(cand))

    need = topk - count(lambda s, k: jnp.where(s > thr, 1.0, 0.0))

    def tie_body(step, last):
        t = last | (jnp.int32(1) << (10 - step))
        cnt = count(lambda s, k: jnp.where(s == thr, jnp.where(k < t, 1.0, 0.0), 0.0))
        return jnp.where(cnt < need, t, last)

    last = lax.fori_loop(0, 11, tie_body, jnp.zeros((1, DSA_Q_BLOCK), jnp.int32))

    m_ref[...] = jnp.full(m_ref.shape, NEG, F32)
    l_ref[...] = jnp.zeros(l_ref.shape, F32)
    acc_ref[...] = jnp.zeros(acc_ref.shape, F32)

    def att_body(c, carry):
        base = pl.multiple_of(c * DSA_KC, DSA_KC)
        s_idx = sc_ref[pl.ds(base, DSA_KC), :]
        kidx = krow + base
        bias = jnp.where(s_idx > thr, 0.0,
                         jnp.where(s_idx == thr, jnp.where(kidx <= last, 0.0, NEG), NEG))
        bias = jnp.where(kidx <= qpos, bias, NEG)
        kc = kk_ref[pl.ds(base, DSA_KC), 0:HEAD_DIM]
        vtc = vt_ref[:, pl.ds(base, DSA_KC)]
        for h in range(DSA_HEADS):
            qh = dq_ref[:, h * HEAD_DIM:(h + 1) * HEAD_DIM]
            s = _nt(kc, qh) + bias
            m_old = m_ref[h:h + 1, :]
            m_new = jnp.maximum(m_old, jnp.max(s, axis=0, keepdims=True))
            alpha = jnp.exp(m_old - m_new)
            p = jnp.exp(s - m_new)
            l_ref[h:h + 1, :] = alpha * l_ref[h:h + 1, :] + jnp.sum(p, axis=0, keepdims=True)
            rows = slice(h * HEAD_DIM, (h + 1) * HEAD_DIM)
            acc_ref[rows, :] = alpha * acc_ref[rows, :] + jnp.dot(vtc, p.astype(BF16), preferred_element_type=F32)
            m_ref[h:h + 1, :] = m_new
        return carry

    lax.fori_loop(0, nch, att_body, 0)

    for pair in range(DSA_HEADS // 2):
        rows = slice(pair * LANES, (pair + 1) * LANES)
        den = jnp.concatenate(
            [jnp.broadcast_to(l_ref[2 * pair:2 * pair + 1, :], (HEAD_DIM, DSA_Q_BLOCK)),
             jnp.broadcast_to(l_ref[2 * pair + 1:2 * pair + 2, :], (HEAD_DIM, DSA_Q_BLOCK))], axis=0)
        y_ref[:, rows] = (acc_ref[rows, :] / den).T.astype(BF16)


def _dsa(iq, kk, dq, vt, iwt):
    bsz, seq, _ = dq.shape
    nq = seq // DSA_Q_BLOCK
    qblk = lambda w: pl.BlockSpec((None, DSA_Q_BLOCK, w), lambda b, i: (b, i, 0))
    return pl.pallas_call(
        _dsa_kernel,
        grid=(bsz, nq),
        in_specs=[qblk(LANES),
                  pl.BlockSpec((None, seq, LANES), lambda b, i: (b, 0, 0)),
                  qblk(DSA_W),
                  pl.BlockSpec((None, HEAD_DIM, seq), lambda b, i: (b, 0, 0)),
                  pl.BlockSpec((None, 16, DSA_Q_BLOCK), lambda b, i: (b, 0, i))],
        out_specs=qblk(DSA_W),
        out_shape=jax.ShapeDtypeStruct((bsz, seq, DSA_W), BF16),
        scratch_shapes=[pltpu.VMEM((seq, DSA_Q_BLOCK), F32),
                        pltpu.VMEM((8, DSA_Q_BLOCK), F32),
                        pltpu.VMEM((8, DSA_Q_BLOCK), F32),
                        pltpu.VMEM((DSA_W, DSA_Q_BLOCK), F32)],
        name="dsa_group",
        compiler_params=_cparams("parallel", "parallel"),
    )(iq, kk, dq, vt, iwt)


def _moba_kernel(mq_ref, mk_ref, mvt_ref, kmean_ref, y_ref, bias_ref, o_ref):
    own = pl.program_id(1)
    blk = MOBA_BLOCK
    nb = kmean_ref.shape[0]
    own_base = pl.multiple_of(own * blk, blk)
    ridx = lax.broadcasted_iota(jnp.int32, (nb, blk), 0)
    causal_bias = jnp.where(lax.broadcasted_iota(jnp.int32, (blk, blk), 0)
                            <= lax.broadcasted_iota(jnp.int32, (blk, blk), 1), 0.0, NEG)

    for h in range(MOBA_HEADS):
        cols = slice(h * HEAD_DIM, (h + 1) * HEAD_DIM)
        qh = mq_ref[:, cols]
        gt = _nt(kmean_ref[:, cols], qh.astype(F32), precision=lax.Precision.HIGHEST)
        for j in range(nb):
            gj = gt[j:j + 1, :]
            beats = jnp.where(gt > gj, 1.0, jnp.where(gt == gj, jnp.where(ridx < j, 1.0, 0.0), 0.0))
            beats = jnp.where(ridx < own, beats, 0.0)
            rank = jnp.sum(beats, axis=0, keepdims=True)
            bias_ref[j:j + 1, :] = jnp.where(rank < float(MOBA_TOPK), 0.0, NEG)

        s = _nt(mk_ref[pl.ds(own_base, blk), cols], qh) + causal_bias
        m0 = jnp.max(s, axis=0, keepdims=True)
        p = jnp.exp(s - m0)
        l0 = jnp.sum(p, axis=0, keepdims=True)
        acc0 = jnp.dot(mvt_ref[cols, pl.ds(own_base, blk)], p.astype(BF16), preferred_element_type=F32)

        def past(j, carry):
            m_old, l_old, acc = carry
            base = pl.multiple_of(j * blk, blk)
            s = _nt(mk_ref[pl.ds(base, blk), cols], qh) + bias_ref[pl.ds(j, 1), :]
            m_new = jnp.maximum(m_old, jnp.max(s, axis=0, keepdims=True))
            alpha = jnp.exp(m_old - m_new)
            p = jnp.exp(s - m_new)
            l_new = alpha * l_old + jnp.sum(p, axis=0, keepdims=True)
            acc = alpha * acc + jnp.dot(mvt_ref[cols, pl.ds(base, blk)], p.astype(BF16), preferred_element_type=F32)
            return m_new, l_new, acc

        _, l_fin, acc = lax.fori_loop(0, own, past, (m0, l0, acc0))
        o_ref[cols, :] = acc / l_fin

    for pair in range(MOBA_HEADS // 2):
        rows = slice(pair * LANES, (pair + 1) * LANES)
        y_ref[:, rows] = o_ref[rows, :].T.astype(BF16)


def _moba(mq, mk, mvt, kmean):
    bsz, seq, _ = mq.shape
    nb = seq // MOBA_BLOCK
    qblk = pl.BlockSpec((None, MOBA_BLOCK, MOBA_W), lambda b, i: (b, i, 0))
    return pl.pallas_call(
        _moba_kernel,
        grid=(bsz, nb),
        in_specs=[qblk,
                  pl.BlockSpec((None, seq, MOBA_W), lambda b, i: (b, 0, 0)),
                  pl.BlockSpec((None, MOBA_W, seq), lambda b, i: (b, 0, 0)),
                  pl.BlockSpec((None, nb, MOBA_W), lambda b, i: (b, 0, 0))],
        out_specs=qblk,
        out_shape=jax.ShapeDtypeStruct((bsz, seq, MOBA_W), BF16),
        scratch_shapes=[pltpu.VMEM((nb, MOBA_BLOCK), F32),
                        pltpu.VMEM((MOBA_W, MOBA_BLOCK), F32)],
        name="moba_group",
        compiler_params=_cparams("parallel", "parallel"),
    )(mq, mk, mvt, kmean)


def _pair_max(vals):
    best = None
    for a in range(len(vals)):
        for b in range(a + 1, len(vals)):
            s = vals[a] + vals[b]
            best = s if best is None else jnp.maximum(best, s)
    return best


def _tail_kernel(x_ref, yc_ref, yd_ref, ym_ref, mod_ref, gffn_ref, wo_ref, wrt_ref, br_ref,
                 wg_ref, wu_ref, wd_ref, gfin_ref, o_ref, gates_ref, *, final_norm):
    tm = x_ref.shape[0]
    mix = (jnp.dot(yc_ref[...], wo_ref[0:CONV_CH, :], preferred_element_type=F32)
           + jnp.dot(yd_ref[...], wo_ref[CONV_CH:CONV_CH + DSA_W, :], preferred_element_type=F32)
           + jnp.dot(ym_ref[...], wo_ref[CONV_CH + DSA_W:MIX_W, :], preferred_element_type=F32))
    x1 = x_ref[...] + mod_ref[2:3, :] * mix
    h = _rms_mod(x1, gffn_ref[...], mod_ref[3:4, :], mod_ref[4:5, :])
    hb = h.astype(BF16)

    aff_t = jax.nn.sigmoid(_nt(wrt_ref[...], h, precision=lax.Precision.HIGHEST))
    bia_t = aff_t + br_ref[...]
    aff = [aff_t[e:e + 1, :] for e in range(N_EXPERTS)]
    bia = [bia_t[e:e + 1, :] for e in range(N_EXPERTS)]
    gscore = [_pair_max(bia[g * EXPERTS_PER_GROUP:(g + 1) * EXPERTS_PER_GROUP]) for g in range(N_EXPERT_GROUPS)]
    best = jnp.zeros((1, tm), jnp.int32)
    best_s = gscore[0]
    for g in range(1, N_EXPERT_GROUPS):
        better = gscore[g] > best_s
        best = jnp.where(better, g, best)
        best_s = jnp.where(better, gscore[g], best_s)
    wsel = []
    for e in range(N_EXPERTS):
        g = e // EXPERTS_PER_GROUP
        rank = jnp.zeros((1, tm), F32)
        for e2 in range(g * EXPERTS_PER_GROUP, (g + 1) * EXPERTS_PER_GROUP):
            if e2 < e:
                rank = rank + jnp.where(bia[e2] >= bia[e], 1.0, 0.0)
            elif e2 > e:
                rank = rank + jnp.where(bia[e2] > bia[e], 1.0, 0.0)
        wsel.append(jnp.where(best == g, jnp.where(rank < float(TOP_K), aff[e], 0.0), 0.0))
    wsum = wsel[0]
    for e in range(1, N_EXPERTS):
        wsum = wsum + wsel[e]
    gates_t = jnp.concatenate([w / wsum for w in wsel]
                              + [jnp.zeros((LANES - N_EXPERTS, tm), F32)], axis=0)
    gates_ref[...] = gates_t.T

    acc = jnp.zeros((tm, D_MODEL), F32)
    for e in range(N_EXPERTS):
        ge = jnp.dot(hb, wg_ref[e], preferred_element_type=F32)
        ue = jnp.dot(hb, wu_ref[e], preferred_element_type=F32)
        a = ge * jax.nn.sigmoid(ge) * ue
        a = (a * gates_ref[:, e:e + 1]).astype(BF16)
        acc = acc + jnp.dot(a, wd_ref[e], preferred_element_type=F32)

    x2 = x1 + mod_ref[5:6, :] * acc
    if final_norm:
        ms = jnp.mean(x2 * x2, axis=-1, keepdims=True)
        x2 = x2 * lax.rsqrt(ms + EPS) * gfin_ref[...]
    o_ref[...] = x2


def _tail(x2d, yc, yd, ym, mod3, g_ffn, w_o, w_rt, b_r, wg, wu, wd, g_fin, seq, final_norm, tm=512):
    ntok, d = x2d.shape
    per_b = seq // tm
    tok = lambda w: pl.BlockSpec((tm, w), lambda t: (t, 0))
    const = lambda shape: pl.BlockSpec(shape, lambda t: (0,) * len(shape), pipeline_mode=pl.Buffered(1))
    return pl.pallas_call(
        functools.partial(_tail_kernel, final_norm=final_norm),
        grid=(ntok // tm,),
        in_specs=[tok(d), tok(CONV_CH), tok(DSA_W), tok(MOBA_W),
                  pl.BlockSpec((None, 6, d), lambda t: (t // per_b, 0, 0)),
                  const((1, d)),
                  const((MIX_W, d)),
                  const((N_EXPERTS, d)),
                  const((N_EXPERTS, 1)),
                  const((N_EXPERTS, d, D_EXPERT)),
                  const((N_EXPERTS, d, D_EXPERT)),
                  const((N_EXPERTS, D_EXPERT, d)),
                  const((1, d))],
        out_specs=tok(d),
        out_shape=jax.ShapeDtypeStruct((ntok, d), F32),
        scratch_shapes=[pltpu.VMEM((tm, LANES), F32)],
        name="tail_moe",
        compiler_params=_cparams("parallel"),
    )(x2d, yc, yd, ym, mod3, g_ffn, w_o, w_rt, b_r, wg, wu, wd, g_fin)


def _rope_tables(seq):
    pos = jnp.arange(seq).astype(F32)[:, None]

    def head_tables(d, lanes):
        rot = d // ROPE_FRACTION
        half = rot // 2
        inv = jnp.exp(-math.log(ROPE_THETA) * (jnp.arange(half, dtype=F32) * 2.0 / rot))
        ang = pos * inv[None, :]
        cos, sin = jnp.cos(ang), jnp.sin(ang)
        ones = jnp.ones((seq, d - rot), F32)
        zeros_h = jnp.zeros((seq, half), F32)
        zeros_r = jnp.zeros((seq, d - rot), F32)
        c = jnp.concatenate([cos, cos, ones], axis=1)
        s1 = jnp.concatenate([-sin, zeros_h, zeros_r], axis=1)
        s2 = jnp.concatenate([zeros_h, sin, zeros_r], axis=1)
        rep = lanes // d
        return [jnp.tile(t, (1, rep)) for t in (c, s1, s2)]

    t64 = head_tables(HEAD_DIM, LANES)
    t32 = head_tables(IDX_DIM, LANES)
    k64 = head_tables(HEAD_DIM, HEAD_DIM)
    k32 = head_tables(IDX_DIM, IDX_DIM)
    pad1 = jnp.ones((seq, LANES - HEAD_DIM - IDX_DIM), F32)
    pad0 = jnp.zeros((seq, LANES - HEAD_DIM - IDX_DIM), F32)
    z64 = jnp.zeros((seq, HEAD_DIM), F32)
    z32 = jnp.zeros((seq, IDX_DIM), F32)
    kk = [jnp.concatenate([k64[0], k32[0], pad1], axis=1),
          jnp.concatenate([k64[1], z32, pad0], axis=1),
          jnp.concatenate([k64[2], z32, pad0], axis=1),
          jnp.concatenate([z64, k32[1], pad0], axis=1),
          jnp.concatenate([z64, k32[2], pad0], axis=1)]
    return jnp.stack(t64 + t32 + kk, axis=0)


def _prep_in_weights(w):
    d = w.shape[0]
    col = lambda name: w[:, _OFF[name][0]:_OFF[name][1]]
    scale = HEAD_DIM ** -0.5
    w_p = jnp.concatenate([
        col("cu"), col("cg"), col("dq") * scale, col("iq"),
        col("dk"), col("ik"), jnp.zeros((d, LANES - HEAD_DIM - IDX_DIM), F32),
        col("mq") * scale, col("mk")], axis=1).astype(BF16)
    w_t = jnp.concatenate([
        col("dv"), col("mv"), col("iw"), jnp.zeros((d, N_PROJ_T - HEAD_DIM - MOBA_W - IDX_HEADS), F32)],
        axis=1).T.astype(BF16)
    return w_p, w_t


def kernel(x, c, w_ada, b_ada, g_mix, w_in, w_dw, b_dw, ln_conv_g, ln_conv_b, w_o, g_ffn, w_router, b_router, w_gate, w_up, w_down, g_final):
    bsz, seq, d = x.shape
    depth = w_ada.shape[0]
    mod = _ada(c, w_ada, b_ada).reshape(depth, bsz, 6, d)
    rope_tabs = _rope_tables(seq)
    w_rt = w_router.T
    b_r = b_router.reshape(N_EXPERTS, 1)
    g_fin = g_final.reshape(1, d)
    for l in range(depth):
        w_p, w_t = _prep_in_weights(w_in[l])
        cucg, dq, iq, kk, mq, mk, kmean, vt, mvt, iwt = _in_proj(
            x, mod[l], g_mix[l].reshape(1, d), w_p, w_t, rope_tabs)
        y_conv = _conv(cucg, w_dw[l], b_dw[l], ln_conv_g[l], ln_conv_b[l])
        y_dsa = _dsa(iq, kk, dq, vt, iwt)
        y_moba = _moba(mq, mk, mvt, kmean.reshape(bsz, seq // MOBA_BLOCK, MOBA_W))
        x = _tail(x.reshape(bsz * seq, d), y_conv.reshape(bsz * seq, CONV_CH),
                  y_dsa.reshape(bsz * seq, DSA_W), y_moba.reshape(bsz * seq, MOBA_W),
                  mod[l], g_ffn[l].reshape(1, d), w_o[l].astype(BF16), w_rt, b_r,
                  w_gate[l].astype(BF16), w_up[l].astype(BF16), w_down[l].astype(BF16), g_fin,
                  seq, final_norm=(l == depth - 1)).reshape(bsz, seq, d)
    return x
```

```python
import functools
import math

import jax
import jax.numpy as jnp
from jax import lax
from jax.experimental import pallas as pl
from jax.experimental.pallas import tpu as pltpu

F32 = jnp.float32
BF16 = jnp.bfloat16

D_MODEL = 1024
HEAD_DIM = 64
CONV_CH = 256
CONV_WIDTH = 31
DSA_HEADS = 6
MOBA_HEADS = 6
DSA_W = DSA_HEADS * HEAD_DIM
MOBA_W = MOBA_HEADS * HEAD_DIM
MIX_W = CONV_CH + DSA_W + MOBA_W
IDX_HEADS = 4
IDX_DIM = 32
DSA_TOPK = 256
DSA_Q_BLOCK = 128
MOBA_BLOCK = 256
MOBA_TOPK = 3
ROPE_THETA = 500000.0
ROPE_FRACTION = 4
N_EXPERTS = 16
N_EXPERT_GROUPS = 4
EXPERTS_PER_GROUP = N_EXPERTS // N_EXPERT_GROUPS
TOP_K = 2
D_EXPERT = 256
EPS = 1e-6

LANES = 128
NEG = -1e30
VMEM_LIMIT = 56 * 1024 * 1024

_OFF = {}
_o = 0
for _name, _w in (("cu", CONV_CH), ("cg", CONV_CH), ("dq", DSA_W), ("dk", HEAD_DIM), ("dv", HEAD_DIM),
                  ("iq", IDX_HEADS * IDX_DIM), ("ik", IDX_DIM), ("iw", IDX_HEADS),
                  ("mq", MOBA_W), ("mk", MOBA_W), ("mv", MOBA_W)):
    _OFF[_name] = (_o, _o + _w)
    _o += _w

C_CUCG = (0, 512)
C_DQ = (512, 896)
C_IQ = (896, 1024)
C_KK = (1024, 1152)
C_MQ = (1152, 1536)
C_MK = (1536, 1920)
N_PROJ = 1920
N_PROJ_T = 464

NT_DIMS = (((1,), (1,)), ((), ()))


def _nt(a, b, precision=None):
    return lax.dot_general(a, b, NT_DIMS, preferred_element_type=F32, precision=precision)


def _cparams(*sem):
    return pltpu.CompilerParams(dimension_semantics=sem, vmem_limit_bytes=VMEM_LIMIT)


def _ada_kernel(c_ref, w_ref, b_ref, o_ref):
    c = c_ref[...]
    ca = c * jax.nn.sigmoid(c)
    o_ref[...] = jnp.dot(ca, w_ref[...], preferred_element_type=F32,
                         precision=lax.Precision.HIGHEST) + b_ref[...]


def _ada(c, w_ada, b_ada):
    depth, d, n = w_ada.shape
    bsz = c.shape[0]
    tn = 1536
    return pl.pallas_call(
        _ada_kernel,
        grid=(depth, n // tn),
        in_specs=[pl.BlockSpec((bsz, d), lambda l, j: (0, 0)),
                  pl.BlockSpec((None, d, tn), lambda l, j: (l, 0, j)),
                  pl.BlockSpec((None, 1, tn), lambda l, j: (l, 0, j))],
        out_specs=pl.BlockSpec((None, bsz, tn), lambda l, j: (l, 0, j)),
        out_shape=jax.ShapeDtypeStruct((depth, bsz, n), F32),
        name="ada_params",
        compiler_params=_cparams("arbitrary", "arbitrary"),
    )(c, w_ada, b_ada.reshape(depth, 1, n))


def _rms_mod(x, g, shift, scale):
    ms = jnp.mean(x * x, axis=-1, keepdims=True)
    h = x * lax.rsqrt(ms + EPS) * g
    return h * (1.0 + scale) + shift


def _in_kernel(x_ref, mod_ref, g_ref, w_ref, wt_ref, rope_ref,
               cucg_ref, dq_ref, iq_ref, kk_ref, mq_ref, mk_ref, kmean_ref, vt_ref, mvt_ref, iwt_ref):
    tm = x_ref.shape[0]
    h = _rms_mod(x_ref[...], g_ref[...], mod_ref[0:1, :], mod_ref[1:2, :])
    hb = h.astype(BF16)

    def proj(cols):
        return jnp.dot(hb, w_ref[:, cols[0]:cols[1]], preferred_element_type=F32)

    def rope(v, t0, half):
        return (v * rope_ref[t0] + pltpu.roll(v, LANES - half, 1) * rope_ref[t0 + 1]
                + pltpu.roll(v, half, 1) * rope_ref[t0 + 2])

    cucg_ref[...] = proj(C_CUCG).astype(BF16)

    p = proj(C_DQ)
    for ch in range(3):
        sl = slice(ch * LANES, (ch + 1) * LANES)
        dq_ref[:, sl] = rope(p[:, sl], 0, 8).astype(BF16)

    iq_ref[...] = rope(proj(C_IQ), 3, 4).astype(BF16)

    p = proj(C_KK)
    kk = (p * rope_ref[6] + pltpu.roll(p, LANES - 8, 1) * rope_ref[7] + pltpu.roll(p, 8, 1) * rope_ref[8]
          + pltpu.roll(p, LANES - 4, 1) * rope_ref[9] + pltpu.roll(p, 4, 1) * rope_ref[10])
    kk_ref[...] = kk.astype(BF16)

    p = proj(C_MQ)
    for ch in range(3):
        sl = slice(ch * LANES, (ch + 1) * LANES)
        mq_ref[:, sl] = rope(p[:, sl], 0, 8).astype(BF16)

    p = proj(C_MK)
    for ch in range(3):
        sl = slice(ch * LANES, (ch + 1) * LANES)
        r = rope(p[:, sl], 0, 8)
        mk_ref[:, sl] = r.astype(BF16)
        for blk in range(tm // MOBA_BLOCK):
            kmean_ref[blk:blk + 1, sl] = jnp.mean(r[blk * MOBA_BLOCK:(blk + 1) * MOBA_BLOCK], axis=0, keepdims=True)

    pt = _nt(wt_ref[...], hb)
    vt_ref[...] = pt[0:HEAD_DIM].astype(BF16)
    mvt_ref[...] = pt[HEAD_DIM:HEAD_DIM + MOBA_W].astype(BF16)
    iwt_ref[...] = pt[HEAD_DIM + MOBA_W:N_PROJ_T]


def _in_proj(x, mod3, g, w_p, w_t, rope_tabs, tm=512):
    bsz, seq, d = x.shape
    nt = seq // tm
    nb = tm // MOBA_BLOCK
    tok = lambda w: pl.BlockSpec((None, tm, w), lambda b, t: (b, t, 0))
    out_shape = (
        jax.ShapeDtypeStruct((bsz, seq, 512), BF16),
        jax.ShapeDtypeStruct((bsz, seq, DSA_W), BF16),
        jax.ShapeDtypeStruct((bsz, seq, LANES), BF16),
        jax.ShapeDtypeStruct((bsz, seq, LANES), BF16),
        jax.ShapeDtypeStruct((bsz, seq, MOBA_W), BF16),
        jax.ShapeDtypeStruct((bsz, seq, MOBA_W), BF16),
        jax.ShapeDtypeStruct((bsz, nt, nb, MOBA_W), F32),
        jax.ShapeDtypeStruct((bsz, HEAD_DIM, seq), BF16),
        jax.ShapeDtypeStruct((bsz, MOBA_W, seq), BF16),
        jax.ShapeDtypeStruct((bsz, 16, seq), F32),
    )
    out_specs = (
        tok(512), tok(DSA_W), tok(LANES), tok(LANES), tok(MOBA_W), tok(MOBA_W),
        pl.BlockSpec((None, None, nb, MOBA_W), lambda b, t: (b, t, 0, 0)),
        pl.BlockSpec((None, HEAD_DIM, tm), lambda b, t: (b, 0, t)),
        pl.BlockSpec((None, MOBA_W, tm), lambda b, t: (b, 0, t)),
        pl.BlockSpec((None, 16, tm), lambda b, t: (b, 0, t)),
    )
    return pl.pallas_call(
        _in_kernel,
        grid=(bsz, nt),
        in_specs=[tok(d),
                  pl.BlockSpec((None, 6, d), lambda b, t: (b, 0, 0)),
                  pl.BlockSpec((1, d), lambda b, t: (0, 0)),
                  pl.BlockSpec((d, N_PROJ), lambda b, t: (0, 0)),
                  pl.BlockSpec((N_PROJ_T, d), lambda b, t: (0, 0)),
                  pl.BlockSpec((11, tm, LANES), lambda b, t: (0, t, 0))],
        out_specs=out_specs,
        out_shape=out_shape,
        name="in_proj",
        compiler_params=_cparams("parallel", "parallel"),
    )(x, mod3, g, w_p, w_t, rope_tabs)


CONV_PAD = 32
CONV_ROWS = 128


def _conv_kernel(cucg_ref, wdw_ref, bdw_ref, lg_ref, lb_ref, y_ref, a_ref):
    seq = cucg_ref.shape[0]
    a_ref[0:CONV_PAD, :] = jnp.zeros((CONV_PAD, CONV_CH), F32)

    def glu(c, carry):
        base = pl.multiple_of(c * 256, 256)
        u = cucg_ref[pl.ds(base, 256), 0:CONV_CH].astype(F32)
        g = cucg_ref[pl.ds(base, 256), CONV_CH:2 * CONV_CH].astype(F32)
        a_ref[pl.ds(base + CONV_PAD, 256), :] = u * jax.nn.sigmoid(g)
        return carry

    lax.fori_loop(0, seq // 256, glu, 0)

    def step(c, carry):
        base = pl.multiple_of(c * CONV_ROWS, CONV_ROWS)
        acc = jnp.zeros((CONV_ROWS, CONV_CH), F32) + bdw_ref[...]
        win = a_ref[pl.ds(base, CONV_ROWS + CONV_PAD), :]
        first = CONV_PAD - (CONV_WIDTH - 1)
        for r in range(8):
            offs = [o for o in range(first, first + CONV_WIDTH) if o % 8 == r]
            shifted = win[r:offs[-1] + CONV_ROWS, :]
            for o in offs:
                acc = acc + shifted[o - r:o - r + CONV_ROWS, :] * wdw_ref[o - first:o - first + 1, :]
        mu = jnp.mean(acc, axis=-1, keepdims=True)
        yc = acc - mu
        var = jnp.mean(yc * yc, axis=-1, keepdims=True)
        yn = yc * lax.rsqrt(var + EPS) * lg_ref[...] + lb_ref[...]
        y_ref[pl.ds(base, CONV_ROWS), :] = (yn * jax.nn.sigmoid(yn)).astype(BF16)
        return carry

    lax.fori_loop(0, seq // CONV_ROWS, step, 0)


def _conv(cucg, w_dw, b_dw, ln_g, ln_b):
    bsz, seq, _ = cucg.shape
    row = pl.BlockSpec((1, CONV_CH), lambda b: (0, 0))
    return pl.pallas_call(
        _conv_kernel,
        grid=(bsz,),
        in_specs=[pl.BlockSpec((None, seq, 2 * CONV_CH), lambda b: (b, 0, 0)),
                  pl.BlockSpec((CONV_WIDTH, CONV_CH), lambda b: (0, 0)),
                  row, row, row],
        out_specs=pl.BlockSpec((None, seq, CONV_CH), lambda b: (b, 0, 0)),
        out_shape=jax.ShapeDtypeStruct((bsz, seq, CONV_CH), BF16),
        scratch_shapes=[pltpu.VMEM((seq + CONV_PAD, CONV_CH), F32)],
        name="conv_group",
        compiler_params=_cparams("parallel"),
    )(cucg, w_dw, b_dw.reshape(1, -1), ln_g.reshape(1, -1), ln_b.reshape(1, -1))


DSA_KC = 256
INT_MIN = -2 ** 31


def _key_to_float(k):
    b = jnp.where(k < 0, jnp.int32(INT_MIN) - k, k)
    return lax.bitcast_convert_type(b, F32)


def _dsa_kernel(iq_ref, kk_ref, dq_ref, vt_ref, iwt_ref, y_ref, sc_ref, acc_ref, q_ref):
    i = pl.program_id(1)
    nch = (i + 2) // 2
    qpos = i * DSA_Q_BLOCK + lax.broadcasted_iota(jnp.int32, (1, DSA_Q_BLOCK), 1)
    krow = lax.broadcasted_iota(jnp.int32, (DSA_KC, DSA_Q_BLOCK), 0)

    iq = iq_ref[...]
    iqh = [iq[:, h * IDX_DIM:(h + 1) * IDX_DIM] for h in range(IDX_HEADS)]
    wrow = [iwt_ref[h:h + 1, :] * (IDX_HEADS ** -0.5) for h in range(IDX_HEADS)]

    def idx_body(c, carry):
        base = pl.multiple_of(c * DSA_KC, DSA_KC)
        ikc = kk_ref[pl.ds(base, DSA_KC), HEAD_DIM:HEAD_DIM + IDX_DIM]
        sc = jnp.zeros((DSA_KC, DSA_Q_BLOCK), F32)
        for h in range(IDX_HEADS):
            lg = _nt(ikc, iqh[h]) * (IDX_DIM ** -0.5)
            sc = sc + jnp.maximum(lg, 0.0) * wrow[h]
        sc_ref[pl.ds(base, DSA_KC), :] = jnp.where(krow + base <= qpos, sc, -jnp.inf)
        return carry

    lax.fori_loop(0, nch, idx_body, 0)

    def tree_sum8(hit):
        while hit.shape[0] > 8:
            half = hit.shape[0] // 2
            hit = hit[:half] + hit[half:]
        return hit

    def count(pred):
        def body(c, acc):
            base = pl.multiple_of(c * DSA_KC, DSA_KC)
            return acc + tree_sum8(pred(sc_ref[pl.ds(base, DSA_KC), :], krow + base))
        acc = lax.fori_loop(0, nch, body, jnp.zeros((8, DSA_Q_BLOCK), F32))
        return jnp.sum(acc, axis=0, keepdims=True)

    topk = float(DSA_TOPK)
    cnt = count(lambda s, k: jnp.where(s >= 0.0, 1.0, 0.0))
    cand0 = jnp.where(cnt >= topk, jnp.int32(0), jnp.int32(INT_MIN))

    def bit_body(step, cand):
        t = cand | (jnp.int32(1) << (30 - step))
        tf = _key_to_float(t)
        cnt = count(lambda s, k: jnp.where(s >= tf, 1.0, 0.0))
        return jnp.where(cnt >= topk, t, cand)

    cand = lax.fori_loop(0, 31, bit_body, cand0)
    thr = jnp.where(cand == jnp.int32(INT_MIN), -jnp.inf, _key_to_float(cand))

    idx_bits = (kk_ref.shape[0] - 1).bit_length()
    need = topk - count(lambda s, k: jnp.where(s > thr, 1.0, 0.0))
    n_tie = count(lambda s, k: jnp.where(s == thr, 1.0, 0.0))

    def tie_body(step, last):
        t = last | (jnp.int32(1) << (idx_bits - 1 - step))
        cnt = count(lambda s, k: jnp.where(s == thr, jnp.where(k < t, 1.0, 0.0), 0.0))
        return jnp.where(cnt < need, t, last)

    def tie_search():
        return lax.fori_loop(0, idx_bits, tie_body, jnp.zeros((1, DSA_Q_BLOCK), jnp.int32))

    surplus = jnp.max(jnp.where(thr > -jnp.inf, n_tie - need, 0.0))
    last = lax.cond(surplus > 0.0, tie_search,
                    lambda: jnp.full((1, DSA_Q_BLOCK), 2 ** idx_bits - 1, jnp.int32))

    acc_ref[...] = jnp.zeros(acc_ref.shape, F32)
    for h in range(DSA_HEADS):
        q_ref[h * DSA_Q_BLOCK:(h + 1) * DSA_Q_BLOCK, :] = dq_ref[:, h * HEAD_DIM:(h + 1) * HEAD_DIM]
    wide = DSA_HEADS * DSA_Q_BLOCK

    def att_body(c, carry):
        m_old, l_old = carry
        base = pl.multiple_of(c * DSA_KC, DSA_KC)
        s_idx = sc_ref[pl.ds(base, DSA_KC), :]
        kidx = krow + base
        bias = jnp.where(s_idx > thr, 0.0,
                         jnp.where(s_idx == thr, jnp.where(kidx <= last, 0.0, NEG), NEG))
        bias = jnp.where(kidx <= qpos, bias, NEG)
        s = _nt(kk_ref[pl.ds(base, DSA_KC), 0:HEAD_DIM], q_ref[...])
        s = s + jnp.concatenate([bias] * DSA_HEADS, axis=1)
        m_new = jnp.maximum(m_old, jnp.max(s, axis=0, keepdims=True))
        alpha = jnp.exp(m_old - m_new)
        p = jnp.exp(s - m_new)
        l_new = alpha * l_old + jnp.sum(p, axis=0, keepdims=True)
        pb = p.astype(BF16)
        vtc = vt_ref[:, pl.ds(base, DSA_KC)]
        pv = jnp.concatenate(
            [jnp.dot(vtc, pb[:, h * DSA_Q_BLOCK:(h + 1) * DSA_Q_BLOCK], preferred_element_type=F32)
             for h in range(DSA_HEADS)], axis=1)
        acc_ref[...] = alpha * acc_ref[...] + pv
        return m_new, l_new

    _, l_fin = lax.fori_loop(0, nch, att_body,
                             (jnp.full((1, wide), NEG, F32), jnp.zeros((1, wide), F32)))

    out = acc_ref[...] / l_fin
    for pair in range(DSA_HEADS // 2):
        two = jnp.concatenate([out[:, (2 * pair) * DSA_Q_BLOCK:(2 * pair + 1) * DSA_Q_BLOCK],
                               out[:, (2 * pair + 1) * DSA_Q_BLOCK:(2 * pair + 2) * DSA_Q_BLOCK]], axis=0)
        y_ref[:, pair * LANES:(pair + 1) * LANES] = two.T.astype(BF16)


def _dsa(iq, kk, dq, vt, iwt):
    bsz, seq, _ = dq.shape
    nq = seq // DSA_Q_BLOCK
    qblk = lambda w: pl.BlockSpec((None, DSA_Q_BLOCK, w), lambda b, i: (b, i, 0))
    return pl.pallas_call(
        _dsa_kernel,
        grid=(bsz, nq),
        in_specs=[qblk(LANES),
                  pl.BlockSpec((None, seq, LANES), lambda b, i: (b, 0, 0)),
                  qblk(DSA_W),
                  pl.BlockSpec((None, HEAD_DIM, seq), lambda b, i: (b, 0, 0)),
                  pl.BlockSpec((None, 16, DSA_Q_BLOCK), lambda b, i: (b, 0, i))],
        out_specs=qblk(DSA_W),
        out_shape=jax.ShapeDtypeStruct((bsz, seq, DSA_W), BF16),
        scratch_shapes=[pltpu.VMEM((seq, DSA_Q_BLOCK), F32),
                        pltpu.VMEM((HEAD_DIM, DSA_HEADS * DSA_Q_BLOCK), F32),
                        pltpu.VMEM((DSA_HEADS * DSA_Q_BLOCK, HEAD_DIM), BF16)],
        name="dsa_group",
        compiler_params=_cparams("parallel", "parallel"),
    )(iq, kk, dq, vt, iwt)


def _moba_kernel(mq_ref, mk_ref, mvt_ref, kmean_ref, y_ref, bias_ref, o_ref):
    own = pl.program_id(1)
    blk = MOBA_BLOCK
    nb = kmean_ref.shape[0]
    own_base = pl.multiple_of(own * blk, blk)
    heads = range(MOBA_HEADS)
    cols = [slice(h * HEAD_DIM, (h + 1) * HEAD_DIM) for h in heads]
    lanes = [slice(h * blk, (h + 1) * blk) for h in heads]
    wide = MOBA_HEADS * blk

    def scores(base):
        return jnp.concatenate([_nt(mk_ref[pl.ds(base, blk), cols[h]], mq_ref[:, cols[h]]) for h in heads], axis=1)

    def weighted_values(base, pb):
        return jnp.concatenate(
            [jnp.dot(mvt_ref[cols[h], pl.ds(base, blk)], pb[:, lanes[h]], preferred_element_type=F32)
             for h in heads], axis=1)

    gt = jnp.concatenate([_nt(kmean_ref[:, cols[h]], mq_ref[:, cols[h]].astype(F32),
                              precision=lax.Precision.HIGHEST) for h in heads], axis=1)
    ridx = lax.broadcasted_iota(jnp.int32, (nb, wide), 0)
    for j in range(nb):
        gj = gt[j:j + 1, :]
        beats = jnp.where(gt > gj, 1.0, jnp.where(gt == gj, jnp.where(ridx < j, 1.0, 0.0), 0.0))
        beats = jnp.where(ridx < own, beats, 0.0)
        rank = jnp.sum(beats, axis=0, keepdims=True)
        bias_ref[j:j + 1, :] = jnp.where(rank < float(MOBA_TOPK), 0.0, NEG)

    causal_bias = jnp.where(lax.broadcasted_iota(jnp.int32, (blk, blk), 0)
                            <= lax.broadcasted_iota(jnp.int32, (blk, blk), 1), 0.0, NEG)
    s = scores(own_base) + jnp.concatenate([causal_bias] * MOBA_HEADS, axis=1)
    m0 = jnp.max(s, axis=0, keepdims=True)
    p = jnp.exp(s - m0)
    l0 = jnp.sum(p, axis=0, keepdims=True)
    o_ref[...] = weighted_values(own_base, p.astype(BF16))

    def past(j, carry):
        m_old, l_old = carry
        base = pl.multiple_of(j * blk, blk)
        s = scores(base) + bias_ref[pl.ds(j, 1), :]
        m_new = jnp.maximum(m_old, jnp.max(s, axis=0, keepdims=True))
        alpha = jnp.exp(m_old - m_new)
        p = jnp.exp(s - m_new)
        l_new = alpha * l_old + jnp.sum(p, axis=0, keepdims=True)
        o_ref[...] = alpha * o_ref[...] + weighted_values(base, p.astype(BF16))
        return m_new, l_new

    _, l_fin = lax.fori_loop(0, own, past, (m0, l0))

    out = o_ref[...] / l_fin
    for pair in range(MOBA_HEADS // 2):
        two = jnp.concatenate([out[:, lanes[2 * pair]], out[:, lanes[2 * pair + 1]]], axis=0)
        y_ref[:, pair * LANES:(pair + 1) * LANES] = two.T.astype(BF16)


def _moba(mq, mk, mvt, kmean):
    bsz, seq, _ = mq.shape
    nb = seq // MOBA_BLOCK
    qblk = pl.BlockSpec((None, MOBA_BLOCK, MOBA_W), lambda b, i: (b, i, 0))
    return pl.pallas_call(
        _moba_kernel,
        grid=(bsz, nb),
        in_specs=[qblk,
                  pl.BlockSpec((None, seq, MOBA_W), lambda b, i: (b, 0, 0)),
                  pl.BlockSpec((None, MOBA_W, seq), lambda b, i: (b, 0, 0)),
                  pl.BlockSpec((None, nb, MOBA_W), lambda b, i: (b, 0, 0))],
        out_specs=qblk,
        out_shape=jax.ShapeDtypeStruct((bsz, seq, MOBA_W), BF16),
        scratch_shapes=[pltpu.VMEM((nb, MOBA_HEADS * MOBA_BLOCK), F32),
                        pltpu.VMEM((HEAD_DIM, MOBA_HEADS * MOBA_BLOCK), F32)],
        name="moba_group",
        compiler_params=_cparams("parallel", "parallel"),
    )(mq, mk, mvt, kmean)


def _pair_max(vals):
    best = None
    for a in range(len(vals)):
        for b in range(a + 1, len(vals)):
            s = vals[a] + vals[b]
            best = s if best is None else jnp.maximum(best, s)
    return best


def _tail_kernel(x_ref, yc_ref, yd_ref, ym_ref, mod_ref, gffn_ref, wo_ref, wrt_ref, br_ref,
                 wg_ref, wu_ref, wd_ref, gfin_ref, o_ref, gates_ref, *, final_norm):
    tm = x_ref.shape[0]
    mix = (jnp.dot(yc_ref[...], wo_ref[0:CONV_CH, :], preferred_element_type=F32)
           + jnp.dot(yd_ref[...], wo_ref[CONV_CH:CONV_CH + DSA_W, :], preferred_element_type=F32)
           + jnp.dot(ym_ref[...], wo_ref[CONV_CH + DSA_W:MIX_W, :], preferred_element_type=F32))
    x1 = x_ref[...] + mod_ref[2:3, :] * mix
    h = _rms_mod(x1, gffn_ref[...], mod_ref[3:4, :], mod_ref[4:5, :])
    hb = h.astype(BF16)

    aff_t = jax.nn.sigmoid(_nt(wrt_ref[...], h, precision=lax.Precision.HIGHEST))
    bia_t = aff_t + br_ref[...]
    aff = [aff_t[e:e + 1, :] for e in range(N_EXPERTS)]
    bia = [bia_t[e:e + 1, :] for e in range(N_EXPERTS)]
    gscore = [_pair_max(bia[g * EXPERTS_PER_GROUP:(g + 1) * EXPERTS_PER_GROUP]) for g in range(N_EXPERT_GROUPS)]
    best = jnp.zeros((1, tm), jnp.int32)
    best_s = gscore[0]
    for g in range(1, N_EXPERT_GROUPS):
        better = gscore[g] > best_s
        best = jnp.where(better, g, best)
        best_s = jnp.where(better, gscore[g], best_s)
    wsel = []
    for e in range(N_EXPERTS):
        g = e // EXPERTS_PER_GROUP
        rank = jnp.zeros((1, tm), F32)
        for e2 in range(g * EXPERTS_PER_GROUP, (g + 1) * EXPERTS_PER_GROUP):
            if e2 < e:
                rank = rank + jnp.where(bia[e2] >= bia[e], 1.0, 0.0)
            elif e2 > e:
                rank = rank + jnp.where(bia[e2] > bia[e], 1.0, 0.0)
        wsel.append(jnp.where(best == g, jnp.where(rank < float(TOP_K), aff[e], 0.0), 0.0))
    wsum = wsel[0]
    for e in range(1, N_EXPERTS):
        wsum = wsum + wsel[e]
    gates_t = jnp.concatenate([w / wsum for w in wsel]
                              + [jnp.zeros((LANES - N_EXPERTS, tm), F32)], axis=0)
    gates_ref[...] = gates_t.T

    acc = jnp.zeros((tm, D_MODEL), F32)
    for e in range(N_EXPERTS):
        ge = jnp.dot(hb, wg_ref[e], preferred_element_type=F32)
        ue = jnp.dot(hb, wu_ref[e], preferred_element_type=F32)
        a = ge * jax.nn.sigmoid(ge) * ue
        a = (a * gates_ref[:, e:e + 1]).astype(BF16)
        acc = acc + jnp.dot(a, wd_ref[e], preferred_element_type=F32)

    x2 = x1 + mod_ref[5:6, :] * acc
    if final_norm:
        ms = jnp.mean(x2 * x2, axis=-1, keepdims=True)
        x2 = x2 * lax.rsqrt(ms + EPS) * gfin_ref[...]
    o_ref[...] = x2


def _tail(x2d, yc, yd, ym, mod3, g_ffn, w_o, w_rt, b_r, wg, wu, wd, g_fin, seq, final_norm, tm=512):
    ntok, d = x2d.shape
    per_b = seq // tm
    tok = lambda w: pl.BlockSpec((tm, w), lambda t: (t, 0))
    const = lambda shape: pl.BlockSpec(shape, lambda t: (0,) * len(shape), pipeline_mode=pl.Buffered(1))
    return pl.pallas_call(
        functools.partial(_tail_kernel, final_norm=final_norm),
        grid=(ntok // tm,),
        in_specs=[tok(d), tok(CONV_CH), tok(DSA_W), tok(MOBA_W),
                  pl.BlockSpec((None, 6, d), lambda t: (t // per_b, 0, 0)),
                  const((1, d)),
                  const((MIX_W, d)),
                  const((N_EXPERTS, d)),
                  const((N_EXPERTS, 1)),
                  const((N_EXPERTS, d, D_EXPERT)),
                  const((N_EXPERTS, d, D_EXPERT)),
                  const((N_EXPERTS, D_EXPERT, d)),
                  const((1, d))],
        out_specs=tok(d),
        out_shape=jax.ShapeDtypeStruct((ntok, d), F32),
        scratch_shapes=[pltpu.VMEM((tm, LANES), F32)],
        name="tail_moe",
        compiler_params=_cparams("parallel"),
    )(x2d, yc, yd, ym, mod3, g_ffn, w_o, w_rt, b_r, wg, wu, wd, g_fin)


def _rope_tables(seq):
    pos = jnp.arange(seq).astype(F32)[:, None]

    def head_tables(d, lanes):
        rot = d // ROPE_FRACTION
        half = rot // 2
        inv = jnp.exp(-math.log(ROPE_THETA) * (jnp.arange(half, dtype=F32) * 2.0 / rot))
        ang = pos * inv[None, :]
        cos, sin = jnp.cos(ang), jnp.sin(ang)
        ones = jnp.ones((seq, d - rot), F32)
        zeros_h = jnp.zeros((seq, half), F32)
        zeros_r = jnp.zeros((seq, d - rot), F32)
        c = jnp.concatenate([cos, cos, ones], axis=1)
        s1 = jnp.concatenate([-sin, zeros_h, zeros_r], axis=1)
        s2 = jnp.concatenate([zeros_h, sin, zeros_r], axis=1)
        rep = lanes // d
        return [jnp.tile(t, (1, rep)) for t in (c, s1, s2)]

    t64 = head_tables(HEAD_DIM, LANES)
    t32 = head_tables(IDX_DIM, LANES)
    k64 = head_tables(HEAD_DIM, HEAD_DIM)
    k32 = head_tables(IDX_DIM, IDX_DIM)
    pad1 = jnp.ones((seq, LANES - HEAD_DIM - IDX_DIM), F32)
    pad0 = jnp.zeros((seq, LANES - HEAD_DIM - IDX_DIM), F32)
    z64 = jnp.zeros((seq, HEAD_DIM), F32)
    z32 = jnp.zeros((seq, IDX_DIM), F32)
    kk = [jnp.concatenate([k64[0], k32[0], pad1], axis=1),
          jnp.concatenate([k64[1], z32, pad0], axis=1),
          jnp.concatenate([k64[2], z32, pad0], axis=1),
          jnp.concatenate([z64, k32[1], pad0], axis=1),
          jnp.concatenate([z64, k32[2], pad0], axis=1)]
    return jnp.stack(t64 + t32 + kk, axis=0)


def _prep_in_weights(w):
    d = w.shape[0]
    col = lambda name: w[:, _OFF[name][0]:_OFF[name][1]]
    scale = HEAD_DIM ** -0.5
    w_p = jnp.concatenate([
        col("cu"), col("cg"), col("dq") * scale, col("iq"),
        col("dk"), col("ik"), jnp.zeros((d, LANES - HEAD_DIM - IDX_DIM), F32),
        col("mq") * scale, col("mk")], axis=1).astype(BF16)
    w_t = jnp.concatenate([
        col("dv"), col("mv"), col("iw"), jnp.zeros((d, N_PROJ_T - HEAD_DIM - MOBA_W - IDX_HEADS), F32)],
        axis=1).T.astype(BF16)
    return w_p, w_t


def kernel(x, c, w_ada, b_ada, g_mix, w_in, w_dw, b_dw, ln_conv_g, ln_conv_b, w_o, g_ffn, w_router, b_router, w_gate, w_up, w_down, g_final):
    bsz, seq, d = x.shape
    depth = w_ada.shape[0]
    mod = _ada(c, w_ada, b_ada).reshape(depth, bsz, 6, d)
    rope_tabs = _rope_tables(seq)
    w_rt = w_router.T
    b_r = b_router.reshape(N_EXPERTS, 1)
    g_fin = g_final.reshape(1, d)
    for l in range(depth):
        w_p, w_t = _prep_in_weights(w_in[l])
        cucg, dq, iq, kk, mq, mk, kmean, vt, mvt, iwt = _in_proj(
            x, mod[l], g_mix[l].reshape(1, d), w_p, w_t, rope_tabs)
        y_conv = _conv(cucg, w_dw[l], b_dw[l], ln_conv_g[l], ln_conv_b[l])
        y_dsa = _dsa(iq, kk, dq, vt, iwt)
        y_moba = _moba(mq, mk, mvt, kmean.reshape(bsz, seq // MOBA_BLOCK, MOBA_W))
        x = _tail(x.reshape(bsz * seq, d), y_conv.reshape(bsz * seq, CONV_CH),
                  y_dsa.reshape(bsz * seq, DSA_W), y_moba.reshape(bsz * seq, MOBA_W),
                  mod[l], g_ffn[l].reshape(1, d), w_o[l].astype(BF16), w_rt, b_r,
                  w_gate[l].astype(BF16), w_up[l].astype(BF16), w_down[l].astype(BF16), g_fin,
                  seq, final_norm=(l == depth - 1)).reshape(bsz, seq, d)
    return x
```

```python
import functools
import math

import jax
import jax.numpy as jnp
from jax import lax
from jax.experimental import pallas as pl
from jax.experimental.pallas import tpu as pltpu

F32 = jnp.float32
BF16 = jnp.bfloat16

D_MODEL = 1024
HEAD_DIM = 64
CONV_CH = 256
CONV_WIDTH = 31
DSA_HEADS = 6
MOBA_HEADS = 6
DSA_W = DSA_HEADS * HEAD_DIM
MOBA_W = MOBA_HEADS * HEAD_DIM
MIX_W = CONV_CH + DSA_W + MOBA_W
IDX_HEADS = 4
IDX_DIM = 32
DSA_TOPK = 256
MOBA_BLOCK = 256
MOBA_TOPK = 3
ROPE_THETA = 500000.0
ROPE_FRACTION = 4
N_EXPERTS = 16
N_EXPERT_GROUPS = 4
EXPERTS_PER_GROUP = N_EXPERTS // N_EXPERT_GROUPS
TOP_K = 2
D_EXPERT = 256
EPS = 1e-6

LANES = 128
NEG = -1e30
VMEM_LIMIT = 56 * 1024 * 1024

_OFF = {}
_o = 0
for _name, _w in (("cu", CONV_CH), ("cg", CONV_CH), ("dq", DSA_W), ("dk", HEAD_DIM), ("dv", HEAD_DIM),
                  ("iq", IDX_HEADS * IDX_DIM), ("ik", IDX_DIM), ("iw", IDX_HEADS),
                  ("mq", MOBA_W), ("mk", MOBA_W), ("mv", MOBA_W)):
    _OFF[_name] = (_o, _o + _w)
    _o += _w

C_CUCG = (0, 512)
C_DQ = (512, 896)
C_IQ = (896, 1024)
C_KK = (1024, 1152)
C_MQ = (1152, 1536)
C_MK = (1536, 1920)
N_PROJ = 1920
N_PROJ_T = 464

NT_DIMS = (((1,), (1,)), ((), ()))


def _nt(a, b, precision=None):
    return lax.dot_general(a, b, NT_DIMS, preferred_element_type=F32, precision=precision)


def _cparams(*sem):
    return pltpu.CompilerParams(dimension_semantics=sem, vmem_limit_bytes=VMEM_LIMIT)


def _ada_kernel(c_ref, w_ref, b_ref, o_ref):
    c = c_ref[...]
    ca = c * jax.nn.sigmoid(c)
    o_ref[...] = jnp.dot(ca, w_ref[...], preferred_element_type=F32,
                         precision=lax.Precision.HIGHEST) + b_ref[...]


def _ada(c, w_ada, b_ada):
    depth, d, n = w_ada.shape
    bsz = c.shape[0]
    tn = 1536
    return pl.pallas_call(
        _ada_kernel,
        grid=(depth, n // tn),
        in_specs=[pl.BlockSpec((bsz, d), lambda l, j: (0, 0)),
                  pl.BlockSpec((None, d, tn), lambda l, j: (l, 0, j)),
                  pl.BlockSpec((None, 1, tn), lambda l, j: (l, 0, j))],
        out_specs=pl.BlockSpec((None, bsz, tn), lambda l, j: (l, 0, j)),
        out_shape=jax.ShapeDtypeStruct((depth, bsz, n), F32),
        name="ada_params",
        compiler_params=_cparams("arbitrary", "arbitrary"),
    )(c, w_ada, b_ada.reshape(depth, 1, n))


def _rms_mod(x, g, shift, scale):
    ms = jnp.mean(x * x, axis=-1, keepdims=True)
    h = x * lax.rsqrt(ms + EPS) * g
    return h * (1.0 + scale) + shift


def _in_kernel(x_ref, mod_ref, g_ref, w_ref, wt_ref, rope_ref,
               cucg_ref, dq_ref, iq_ref, kk_ref, mq_ref, mk_ref, kmean_ref, vt_ref, mvt_ref, iwt_ref):
    tm = x_ref.shape[0]
    h = _rms_mod(x_ref[...], g_ref[...], mod_ref[0:1, :], mod_ref[1:2, :])
    hb = h.astype(BF16)

    def proj(cols):
        return jnp.dot(hb, w_ref[:, cols[0]:cols[1]], preferred_element_type=F32)

    def rope(v, t0, half):
        return (v * rope_ref[t0] + pltpu.roll(v, LANES - half, 1) * rope_ref[t0 + 1]
                + pltpu.roll(v, half, 1) * rope_ref[t0 + 2])

    cucg_ref[...] = proj(C_CUCG).astype(BF16)

    p = proj(C_DQ)
    for ch in range(3):
        sl = slice(ch * LANES, (ch + 1) * LANES)
        dq_ref[:, sl] = rope(p[:, sl], 0, 8).astype(BF16)

    iq_ref[...] = rope(proj(C_IQ), 3, 4).astype(BF16)

    p = proj(C_KK)
    kk = (p * rope_ref[6] + pltpu.roll(p, LANES - 8, 1) * rope_ref[7] + pltpu.roll(p, 8, 1) * rope_ref[8]
          + pltpu.roll(p, LANES - 4, 1) * rope_ref[9] + pltpu.roll(p, 4, 1) * rope_ref[10])
    kk_ref[...] = kk.astype(BF16)

    p = proj(C_MQ)
    for ch in range(3):
        sl = slice(ch * LANES, (ch + 1) * LANES)
        mq_ref[:, sl] = rope(p[:, sl], 0, 8).astype(BF16)

    p = proj(C_MK)
    for ch in range(3):
        sl = slice(ch * LANES, (ch + 1) * LANES)
        r = rope(p[:, sl], 0, 8)
        mk_ref[:, sl] = r.astype(BF16)
        for blk in range(tm // MOBA_BLOCK):
            kmean_ref[blk:blk + 1, sl] = jnp.mean(r[blk * MOBA_BLOCK:(blk + 1) * MOBA_BLOCK], axis=0, keepdims=True)

    pt = _nt(wt_ref[...], hb)
    vt_ref[...] = pt[0:HEAD_DIM].astype(BF16)
    mvt_ref[...] = pt[HEAD_DIM:HEAD_DIM + MOBA_W].astype(BF16)
    iwt_ref[...] = pt[HEAD_DIM + MOBA_W:N_PROJ_T]


def _in_proj(x, mod3, g, w_p, w_t, rope_tabs, tm=512):
    bsz, seq, d = x.shape
    nt = seq // tm
    nb = tm // MOBA_BLOCK
    tok = lambda w: pl.BlockSpec((None, tm, w), lambda b, t: (b, t, 0))
    out_shape = (
        jax.ShapeDtypeStruct((bsz, seq, 512), BF16),
        jax.ShapeDtypeStruct((bsz, seq, DSA_W), BF16),
        jax.ShapeDtypeStruct((bsz, seq, LANES), BF16),
        jax.ShapeDtypeStruct((bsz, seq, LANES), BF16),
        jax.ShapeDtypeStruct((bsz, seq, MOBA_W), BF16),
        jax.ShapeDtypeStruct((bsz, seq, MOBA_W), BF16),
        jax.ShapeDtypeStruct((bsz, nt, nb, MOBA_W), F32),
        jax.ShapeDtypeStruct((bsz, HEAD_DIM, seq), BF16),
        jax.ShapeDtypeStruct((bsz, MOBA_W, seq), BF16),
        jax.ShapeDtypeStruct((bsz, 16, seq), F32),
    )
    out_specs = (
        tok(512), tok(DSA_W), tok(LANES), tok(LANES), tok(MOBA_W), tok(MOBA_W),
        pl.BlockSpec((None, None, nb, MOBA_W), lambda b, t: (b, t, 0, 0)),
        pl.BlockSpec((None, HEAD_DIM, tm), lambda b, t: (b, 0, t)),
        pl.BlockSpec((None, MOBA_W, tm), lambda b, t: (b, 0, t)),
        pl.BlockSpec((None, 16, tm), lambda b, t: (b, 0, t)),
    )
    return pl.pallas_call(
        _in_kernel,
        grid=(bsz, nt),
        in_specs=[tok(d),
                  pl.BlockSpec((None, 6, d), lambda b, t: (b, 0, 0)),
                  pl.BlockSpec((1, d), lambda b, t: (0, 0)),
                  pl.BlockSpec((d, N_PROJ), lambda b, t: (0, 0)),
                  pl.BlockSpec((N_PROJ_T, d), lambda b, t: (0, 0)),
                  pl.BlockSpec((11, tm, LANES), lambda b, t: (0, t, 0))],
        out_specs=out_specs,
        out_shape=out_shape,
        name="in_proj",
        compiler_params=_cparams("parallel", "parallel"),
    )(x, mod3, g, w_p, w_t, rope_tabs)


CONV_PAD = 32
CONV_ROWS = 128


def _conv_kernel(cucg_ref, wdw_ref, bdw_ref, lg_ref, lb_ref, y_ref, a_ref):
    seq = cucg_ref.shape[0]
    a_ref[0:CONV_PAD, :] = jnp.zeros((CONV_PAD, CONV_CH), F32)

    def glu(c, carry):
        base = pl.multiple_of(c * 256, 256)
        u = cucg_ref[pl.ds(base, 256), 0:CONV_CH].astype(F32)
        g = cucg_ref[pl.ds(base, 256), CONV_CH:2 * CONV_CH].astype(F32)
        a_ref[pl.ds(base + CONV_PAD, 256), :] = u * jax.nn.sigmoid(g)
        return carry

    lax.fori_loop(0, seq // 256, glu, 0)

    def step(c, carry):
        base = pl.multiple_of(c * CONV_ROWS, CONV_ROWS)
        acc = jnp.zeros((CONV_ROWS, CONV_CH), F32) + bdw_ref[...]
        win = a_ref[pl.ds(base, CONV_ROWS + CONV_PAD), :]
        first = CONV_PAD - (CONV_WIDTH - 1)
        for r in range(8):
            offs = [o for o in range(first, first + CONV_WIDTH) if o % 8 == r]
            shifted = win[r:offs[-1] + CONV_ROWS, :]
            for o in offs:
                acc = acc + shifted[o - r:o - r + CONV_ROWS, :] * wdw_ref[o - first:o - first + 1, :]
        mu = jnp.mean(acc, axis=-1, keepdims=True)
        yc = acc - mu
        var = jnp.mean(yc * yc, axis=-1, keepdims=True)
        yn = yc * lax.rsqrt(var + EPS) * lg_ref[...] + lb_ref[...]
        y_ref[pl.ds(base, CONV_ROWS), :] = (yn * jax.nn.sigmoid(yn)).astype(BF16)
        return carry

    lax.fori_loop(0, seq // CONV_ROWS, step, 0)


def _conv(cucg, w_dw, b_dw, ln_g, ln_b):
    bsz, seq, _ = cucg.shape
    row = pl.BlockSpec((1, CONV_CH), lambda b: (0, 0))
    return pl.pallas_call(
        _conv_kernel,
        grid=(bsz,),
        in_specs=[pl.BlockSpec((None, seq, 2 * CONV_CH), lambda b: (b, 0, 0)),
                  pl.BlockSpec((CONV_WIDTH, CONV_CH), lambda b: (0, 0)),
                  row, row, row],
        out_specs=pl.BlockSpec((None, seq, CONV_CH), lambda b: (b, 0, 0)),
        out_shape=jax.ShapeDtypeStruct((bsz, seq, CONV_CH), BF16),
        scratch_shapes=[pltpu.VMEM((seq + CONV_PAD, CONV_CH), F32)],
        name="conv_group",
        compiler_params=_cparams("parallel"),
    )(cucg, w_dw, b_dw.reshape(1, -1), ln_g.reshape(1, -1), ln_b.reshape(1, -1))


DSA_KC = 256
DSA_TQ = 256
INT_MIN = -2 ** 31


def _key_to_float(k):
    b = jnp.where(k < 0, jnp.int32(INT_MIN) - k, k)
    return lax.bitcast_convert_type(b, F32)


def _dsa_kernel(iq_ref, kk_ref, dq_ref, vt_ref, iwt_ref, y_ref, sc_ref, acc_ref, q_ref, iqw_ref, sa_ref, sb_ref):
    i = pl.program_id(1)
    nch = ((i + 1) * DSA_TQ + DSA_KC - 1) // DSA_KC
    qpos = i * DSA_TQ + lax.broadcasted_iota(jnp.int32, (1, DSA_TQ), 1)
    krow = lax.broadcasted_iota(jnp.int32, (DSA_KC, DSA_TQ), 0)

    for h in range(IDX_HEADS):
        iqw_ref[h * DSA_TQ:(h + 1) * DSA_TQ, :] = iq_ref[:, h * IDX_DIM:(h + 1) * IDX_DIM]
    wwide = jnp.concatenate([iwt_ref[h:h + 1, :] for h in range(IDX_HEADS)], axis=1) * (IDX_HEADS ** -0.5)

    def idx_body(c, carry):
        base = pl.multiple_of(c * DSA_KC, DSA_KC)
        lg = _nt(kk_ref[pl.ds(base, DSA_KC), HEAD_DIM:HEAD_DIM + IDX_DIM], iqw_ref[...])
        r = jnp.maximum(lg * (IDX_DIM ** -0.5), 0.0) * wwide
        sc = r[:, 0:DSA_TQ]
        for h in range(1, IDX_HEADS):
            sc = sc + r[:, h * DSA_TQ:(h + 1) * DSA_TQ]
        sc_ref[pl.ds(base, DSA_KC), :] = jnp.where(krow + base <= qpos, sc, -jnp.inf)
        return carry

    lax.fori_loop(0, nch, idx_body, 0)

    def tree_sum8(hit):
        while hit.shape[0] > 8:
            half = hit.shape[0] // 2
            hit = hit[:half] + hit[half:]
        return hit

    def count(pred):
        def body(c, acc):
            base = pl.multiple_of(c * DSA_KC, DSA_KC)
            return acc + tree_sum8(pred(sc_ref[pl.ds(base, DSA_KC), :], krow + base))
        acc = lax.fori_loop(0, nch, body, jnp.zeros((8, DSA_TQ), F32))
        return jnp.sum(acc, axis=0, keepdims=True)

    topk = float(DSA_TOPK)
    cnt = count(lambda s, k: jnp.where(s >= 0.0, 1.0, 0.0))
    cand0 = jnp.where(cnt >= topk, jnp.int32(0), jnp.int32(INT_MIN))

    def bit_body(step, cand):
        t = cand | (jnp.int32(1) << (30 - step))
        tf = _key_to_float(t)
        cnt = count(lambda s, k: jnp.where(s >= tf, 1.0, 0.0))
        return jnp.where(cnt >= topk, t, cand)

    cand = lax.fori_loop(0, 31, bit_body, cand0)
    thr = jnp.where(cand == jnp.int32(INT_MIN), -jnp.inf, _key_to_float(cand))

    idx_bits = (kk_ref.shape[0] - 1).bit_length()
    need = topk - count(lambda s, k: jnp.where(s > thr, 1.0, 0.0))
    n_tie = count(lambda s, k: jnp.where(s == thr, 1.0, 0.0))

    def tie_body(step, last):
        t = last | (jnp.int32(1) << (idx_bits - 1 - step))
        cnt = count(lambda s, k: jnp.where(s == thr, jnp.where(k < t, 1.0, 0.0), 0.0))
        return jnp.where(cnt < need, t, last)

    def tie_search():
        return lax.fori_loop(0, idx_bits, tie_body, jnp.zeros((1, DSA_TQ), jnp.int32))

    surplus = jnp.max(jnp.where(thr > -jnp.inf, n_tie - need, 0.0))
    last = lax.cond(surplus > 0.0, tie_search,
                    lambda: jnp.full((1, DSA_TQ), 2 ** idx_bits - 1, jnp.int32))

    acc_ref[...] = jnp.zeros(acc_ref.shape, F32)
    for h in range(DSA_HEADS):
        q_ref[h * DSA_TQ:(h + 1) * DSA_TQ, :] = dq_ref[:, h * HEAD_DIM:(h + 1) * HEAD_DIM]
    wide = DSA_HEADS * DSA_TQ
    half_rows = DSA_KC // 2
    ones_rows = jnp.ones((16, half_rows), BF16)
    hrow = lax.broadcasted_iota(jnp.int32, (half_rows, DSA_TQ), 0)

    def masked_scores(c, half):
        base = pl.multiple_of(c * DSA_KC + half * half_rows, half_rows)
        s_idx = sc_ref[pl.ds(base, half_rows), :]
        kidx = hrow + base
        bias = jnp.where(s_idx > thr, 0.0,
                         jnp.where(s_idx == thr, jnp.where(kidx <= last, 0.0, NEG), NEG))
        bias = jnp.where(kidx <= qpos, bias, NEG)
        s = _nt(kk_ref[pl.ds(base, half_rows), 0:HEAD_DIM], q_ref[...])
        return s + jnp.concatenate([bias] * DSA_HEADS, axis=1)

    def softmax_step(s, m_old, l_old, c, half):
        m_new = jnp.maximum(m_old, jnp.max(s, axis=0, keepdims=True))
        alpha = jnp.exp2(m_old - m_new)
        pb = jnp.exp2((s - m_new).astype(BF16))
        l_new = alpha * l_old + jnp.dot(ones_rows, pb, preferred_element_type=F32)[0:1, :]
        vtc = vt_ref[:, pl.ds(pl.multiple_of(c * DSA_KC + half * half_rows, half_rows), half_rows)]
        pv = jnp.concatenate(
            [jnp.dot(vtc, pb[:, h * DSA_TQ:(h + 1) * DSA_TQ], preferred_element_type=F32)
             for h in range(DSA_HEADS)], axis=1)
        acc_ref[...] = alpha * acc_ref[...] + pv
        return m_new, l_new

    sa_ref[...] = masked_scores(0, 0)

    def att_body(c, carry):
        m, l = carry
        sb_ref[...] = masked_scores(c, 1)
        m, l = softmax_step(sa_ref[...], m, l, c, 0)
        sa_ref[...] = masked_scores(jnp.minimum(c + 1, nch - 1), 0)
        return softmax_step(sb_ref[...], m, l, c, 1)

    _, l_fin = lax.fori_loop(0, nch, att_body,
                             (jnp.full((1, wide), NEG, F32), jnp.zeros((1, wide), F32)))

    out = acc_ref[...] / l_fin
    for pair in range(DSA_HEADS // 2):
        two = jnp.concatenate([out[:, (2 * pair) * DSA_TQ:(2 * pair + 1) * DSA_TQ],
                               out[:, (2 * pair + 1) * DSA_TQ:(2 * pair + 2) * DSA_TQ]], axis=0)
        y_ref[:, pair * LANES:(pair + 1) * LANES] = two.T.astype(BF16)


def _dsa(iq, kk, dq, vt, iwt):
    bsz, seq, _ = dq.shape
    nq = seq // DSA_TQ
    qblk = lambda w: pl.BlockSpec((None, DSA_TQ, w), lambda b, i: (b, i, 0))
    return pl.pallas_call(
        _dsa_kernel,
        grid=(bsz, nq),
        in_specs=[qblk(LANES),
                  pl.BlockSpec((None, seq, LANES), lambda b, i: (b, 0, 0)),
                  qblk(DSA_W),
                  pl.BlockSpec((None, HEAD_DIM, seq), lambda b, i: (b, 0, 0)),
                  pl.BlockSpec((None, 16, DSA_TQ), lambda b, i: (b, 0, i))],
        out_specs=qblk(DSA_W),
        out_shape=jax.ShapeDtypeStruct((bsz, seq, DSA_W), BF16),
        scratch_shapes=[pltpu.VMEM((seq, DSA_TQ), F32),
                        pltpu.VMEM((HEAD_DIM, DSA_HEADS * DSA_TQ), F32),
                        pltpu.VMEM((DSA_HEADS * DSA_TQ, HEAD_DIM), BF16),
                        pltpu.VMEM((IDX_HEADS * DSA_TQ, IDX_DIM), BF16),
                        pltpu.VMEM((DSA_KC // 2, DSA_HEADS * DSA_TQ), F32),
                        pltpu.VMEM((DSA_KC // 2, DSA_HEADS * DSA_TQ), F32)],
        name="dsa_group",
        compiler_params=_cparams("parallel", "parallel"),
    )(iq, kk, dq, vt, iwt)


def _moba_kernel(mq_ref, mk_ref, mvt_ref, kmean_ref, y_ref, bias_ref, o_ref):
    own = pl.program_id(1)
    blk = MOBA_BLOCK
    nb = kmean_ref.shape[0]
    own_base = pl.multiple_of(own * blk, blk)
    heads = range(MOBA_HEADS)
    cols = [slice(h * HEAD_DIM, (h + 1) * HEAD_DIM) for h in heads]
    lanes = [slice(h * blk, (h + 1) * blk) for h in heads]
    wide = MOBA_HEADS * blk

    def scores(base):
        return jnp.concatenate([_nt(mk_ref[pl.ds(base, blk), cols[h]], mq_ref[:, cols[h]]) for h in heads], axis=1)

    def weighted_values(base, pb):
        return jnp.concatenate(
            [jnp.dot(mvt_ref[cols[h], pl.ds(base, blk)], pb[:, lanes[h]], preferred_element_type=F32)
             for h in heads], axis=1)

    gt = jnp.concatenate([_nt(kmean_ref[:, cols[h]], mq_ref[:, cols[h]].astype(F32),
                              precision=lax.Precision.HIGHEST) for h in heads], axis=1)
    ridx = lax.broadcasted_iota(jnp.int32, (nb, wide), 0)
    for j in range(nb):
        gj = gt[j:j + 1, :]
        beats = jnp.where(gt > gj, 1.0, jnp.where(gt == gj, jnp.where(ridx < j, 1.0, 0.0), 0.0))
        beats = jnp.where(ridx < own, beats, 0.0)
        rank = jnp.sum(beats, axis=0, keepdims=True)
        bias_ref[j:j + 1, :] = jnp.where(rank < float(MOBA_TOPK), 0.0, NEG)

    causal_bias = jnp.where(lax.broadcasted_iota(jnp.int32, (blk, blk), 0)
                            <= lax.broadcasted_iota(jnp.int32, (blk, blk), 1), 0.0, NEG)
    s = scores(own_base) + jnp.concatenate([causal_bias] * MOBA_HEADS, axis=1)
    ones_rows = jnp.ones((16, blk), BF16)
    m0 = jnp.max(s, axis=0, keepdims=True)
    pb = jnp.exp2((s - m0).astype(BF16))
    l0 = jnp.dot(ones_rows, pb, preferred_element_type=F32)[0:1, :]
    o_ref[...] = weighted_values(own_base, pb)

    def past(j, carry):
        m_old, l_old = carry
        base = pl.multiple_of(j * blk, blk)
        s = scores(base) + bias_ref[pl.ds(j, 1), :]
        m_new = jnp.maximum(m_old, jnp.max(s, axis=0, keepdims=True))
        alpha = jnp.exp2(m_old - m_new)
        pb = jnp.exp2((s - m_new).astype(BF16))
        l_new = alpha * l_old + jnp.dot(ones_rows, pb, preferred_element_type=F32)[0:1, :]
        o_ref[...] = alpha * o_ref[...] + weighted_values(base, pb)
        return m_new, l_new

    _, l_fin = lax.fori_loop(0, own, past, (m0, l0))

    out = o_ref[...] / l_fin
    for pair in range(MOBA_HEADS // 2):
        two = jnp.concatenate([out[:, lanes[2 * pair]], out[:, lanes[2 * pair + 1]]], axis=0)
        y_ref[:, pair * LANES:(pair + 1) * LANES] = two.T.astype(BF16)


def _moba(mq, mk, mvt, kmean):
    bsz, seq, _ = mq.shape
    nb = seq // MOBA_BLOCK
    qblk = pl.BlockSpec((None, MOBA_BLOCK, MOBA_W), lambda b, i: (b, i, 0))
    return pl.pallas_call(
        _moba_kernel,
        grid=(bsz, nb),
        in_specs=[qblk,
                  pl.BlockSpec((None, seq, MOBA_W), lambda b, i: (b, 0, 0)),
                  pl.BlockSpec((None, MOBA_W, seq), lambda b, i: (b, 0, 0)),
                  pl.BlockSpec((None, nb, MOBA_W), lambda b, i: (b, 0, 0))],
        out_specs=qblk,
        out_shape=jax.ShapeDtypeStruct((bsz, seq, MOBA_W), BF16),
        scratch_shapes=[pltpu.VMEM((nb, MOBA_HEADS * MOBA_BLOCK), F32),
                        pltpu.VMEM((HEAD_DIM, MOBA_HEADS * MOBA_BLOCK), F32)],
        name="moba_group",
        compiler_params=_cparams("parallel", "parallel"),
    )(mq, mk, mvt, kmean)


def _pair_max(vals):
    best = None
    for a in range(len(vals)):
        for b in range(a + 1, len(vals)):
            s = vals[a] + vals[b]
            best = s if best is None else jnp.maximum(best, s)
    return best


def _tail_kernel(x_ref, yc_ref, yd_ref, ym_ref, mod_ref, gffn_ref, wo_ref, wrt_ref, br_ref,
                 wg_ref, wu_ref, wd_ref, gfin_ref, o_ref, gates_ref, *, final_norm):
    tm = x_ref.shape[0]
    mix = (jnp.dot(yc_ref[...], wo_ref[0:CONV_CH, :], preferred_element_type=F32)
           + jnp.dot(yd_ref[...], wo_ref[CONV_CH:CONV_CH + DSA_W, :], preferred_element_type=F32)
           + jnp.dot(ym_ref[...], wo_ref[CONV_CH + DSA_W:MIX_W, :], preferred_element_type=F32))
    x1 = x_ref[...] + mod_ref[2:3, :] * mix
    h = _rms_mod(x1, gffn_ref[...], mod_ref[3:4, :], mod_ref[4:5, :])
    hb = h.astype(BF16)

    aff_t = jax.nn.sigmoid(_nt(wrt_ref[...], h, precision=lax.Precision.HIGHEST))
    bia_t = aff_t + br_ref[...]
    aff = [aff_t[e:e + 1, :] for e in range(N_EXPERTS)]
    bia = [bia_t[e:e + 1, :] for e in range(N_EXPERTS)]
    gscore = [_pair_max(bia[g * EXPERTS_PER_GROUP:(g + 1) * EXPERTS_PER_GROUP]) for g in range(N_EXPERT_GROUPS)]
    best = jnp.zeros((1, tm), jnp.int32)
    best_s = gscore[0]
    for g in range(1, N_EXPERT_GROUPS):
        better = gscore[g] > best_s
        best = jnp.where(better, g, best)
        best_s = jnp.where(better, gscore[g], best_s)
    wsel = []
    for e in range(N_EXPERTS):
        g = e // EXPERTS_PER_GROUP
        rank = jnp.zeros((1, tm), F32)
        for e2 in range(g * EXPERTS_PER_GROUP, (g + 1) * EXPERTS_PER_GROUP):
            if e2 < e:
                rank = rank + jnp.where(bia[e2] >= bia[e], 1.0, 0.0)
            elif e2 > e:
                rank = rank + jnp.where(bia[e2] > bia[e], 1.0, 0.0)
        wsel.append(jnp.where(best == g, jnp.where(rank < float(TOP_K), aff[e], 0.0), 0.0))
    wsum = wsel[0]
    for e in range(1, N_EXPERTS):
        wsum = wsum + wsel[e]
    gates_t = jnp.concatenate([w / wsum for w in wsel]
                              + [jnp.zeros((LANES - N_EXPERTS, tm), F32)], axis=0)
    gates_ref[...] = gates_t.T

    acc = jnp.zeros((tm, D_MODEL), F32)
    for e in range(N_EXPERTS):
        ge = jnp.dot(hb, wg_ref[e], preferred_element_type=F32)
        ue = jnp.dot(hb, wu_ref[e], preferred_element_type=F32)
        a = ge * jax.nn.sigmoid(ge) * ue
        a = (a * gates_ref[:, e:e + 1]).astype(BF16)
        acc = acc + jnp.dot(a, wd_ref[e], preferred_element_type=F32)

    x2 = x1 + mod_ref[5:6, :] * acc
    if final_norm:
        ms = jnp.mean(x2 * x2, axis=-1, keepdims=True)
        x2 = x2 * lax.rsqrt(ms + EPS) * gfin_ref[...]
    o_ref[...] = x2


def _tail(x2d, yc, yd, ym, mod3, g_ffn, w_o, w_rt, b_r, wg, wu, wd, g_fin, seq, final_norm, tm=512):
    ntok, d = x2d.shape
    per_b = seq // tm
    tok = lambda w: pl.BlockSpec((tm, w), lambda t: (t, 0))
    const = lambda shape: pl.BlockSpec(shape, lambda t: (0,) * len(shape), pipeline_mode=pl.Buffered(1))
    return pl.pallas_call(
        functools.partial(_tail_kernel, final_norm=final_norm),
        grid=(ntok // tm,),
        in_specs=[tok(d), tok(CONV_CH), tok(DSA_W), tok(MOBA_W),
                  pl.BlockSpec((None, 6, d), lambda t: (t // per_b, 0, 0)),
                  const((1, d)),
                  const((MIX_W, d)),
                  const((N_EXPERTS, d)),
                  const((N_EXPERTS, 1)),
                  const((N_EXPERTS, d, D_EXPERT)),
                  const((N_EXPERTS, d, D_EXPERT)),
                  const((N_EXPERTS, D_EXPERT, d)),
                  const((1, d))],
        out_specs=tok(d),
        out_shape=jax.ShapeDtypeStruct((ntok, d), F32),
        scratch_shapes=[pltpu.VMEM((tm, LANES), F32)],
        name="tail_moe",
        compiler_params=_cparams("parallel"),
    )(x2d, yc, yd, ym, mod3, g_ffn, w_o, w_rt, b_r, wg, wu, wd, g_fin)


def _rope_tables(seq):
    pos = jnp.arange(seq).astype(F32)[:, None]

    def head_tables(d, lanes):
        rot = d // ROPE_FRACTION
        half = rot // 2
        inv = jnp.exp(-math.log(ROPE_THETA) * (jnp.arange(half, dtype=F32) * 2.0 / rot))
        ang = pos * inv[None, :]
        cos, sin = jnp.cos(ang), jnp.sin(ang)
        ones = jnp.ones((seq, d - rot), F32)
        zeros_h = jnp.zeros((seq, half), F32)
        zeros_r = jnp.zeros((seq, d - rot), F32)
        c = jnp.concatenate([cos, cos, ones], axis=1)
        s1 = jnp.concatenate([-sin, zeros_h, zeros_r], axis=1)
        s2 = jnp.concatenate([zeros_h, sin, zeros_r], axis=1)
        rep = lanes // d
        return [jnp.tile(t, (1, rep)) for t in (c, s1, s2)]

    t64 = head_tables(HEAD_DIM, LANES)
    t32 = head_tables(IDX_DIM, LANES)
    k64 = head_tables(HEAD_DIM, HEAD_DIM)
    k32 = head_tables(IDX_DIM, IDX_DIM)
    pad1 = jnp.ones((seq, LANES - HEAD_DIM - IDX_DIM), F32)
    pad0 = jnp.zeros((seq, LANES - HEAD_DIM - IDX_DIM), F32)
    z64 = jnp.zeros((seq, HEAD_DIM), F32)
    z32 = jnp.zeros((seq, IDX_DIM), F32)
    kk = [jnp.concatenate([k64[0], k32[0], pad1], axis=1),
          jnp.concatenate([k64[1], z32, pad0], axis=1),
          jnp.concatenate([k64[2], z32, pad0], axis=1),
          jnp.concatenate([z64, k32[1], pad0], axis=1),
          jnp.concatenate([z64, k32[2], pad0], axis=1)]
    return jnp.stack(t64 + t32 + kk, axis=0)


def _prep_in_weights(w):
    d = w.shape[0]
    col = lambda name: w[:, _OFF[name][0]:_OFF[name][1]]
    scale = HEAD_DIM ** -0.5 * math.log2(math.e)
    w_p = jnp.concatenate([
        col("cu"), col("cg"), col("dq") * scale, col("iq"),
        col("dk"), col("ik"), jnp.zeros((d, LANES - HEAD_DIM - IDX_DIM), F32),
        col("mq") * scale, col("mk")], axis=1).astype(BF16)
    w_t = jnp.concatenate([
        col("dv"), col("mv"), col("iw"), jnp.zeros((d, N_PROJ_T - HEAD_DIM - MOBA_W - IDX_HEADS), F32)],
        axis=1).T.astype(BF16)
    return w_p, w_t


def kernel(x, c, w_ada, b_ada, g_mix, w_in, w_dw, b_dw, ln_conv_g, ln_conv_b, w_o, g_ffn, w_router, b_router, w_gate, w_up, w_down, g_final):
    bsz, seq, d = x.shape
    depth = w_ada.shape[0]
    mod = _ada(c, w_ada, b_ada).reshape(depth, bsz, 6, d)
    rope_tabs = _rope_tables(seq)
    w_rt = w_router.T
    b_r = b_router.reshape(N_EXPERTS, 1)
    g_fin = g_final.reshape(1, d)
    for l in range(depth):
        w_p, w_t = _prep_in_weights(w_in[l])
        cucg, dq, iq, kk, mq, mk, kmean, vt, mvt, iwt = _in_proj(
            x, mod[l], g_mix[l].reshape(1, d), w_p, w_t, rope_tabs)
        y_conv = _conv(cucg, w_dw[l], b_dw[l], ln_conv_g[l], ln_conv_b[l])
        y_dsa = _dsa(iq, kk, dq, vt, iwt)
        y_moba = _moba(mq, mk, mvt, kmean.reshape(bsz, seq // MOBA_BLOCK, MOBA_W))
        x = _tail(x.reshape(bsz * seq, d), y_conv.reshape(bsz * seq, CONV_CH),
                  y_dsa.reshape(bsz * seq, DSA_W), y_moba.reshape(bsz * seq, MOBA_W),
                  mod[l], g_ffn[l].reshape(1, d), w_o[l].astype(BF16), w_rt, b_r,
                  w_gate[l].astype(BF16), w_up[l].astype(BF16), w_down[l].astype(BF16), g_fin,
                  seq, final_norm=(l == depth - 1)).reshape(bsz, seq, d)
    return x
```

```python
import functools
import math

import jax
import jax.numpy as jnp
import numpy as np
from jax import lax
from jax.experimental import pallas as pl
from jax.experimental.pallas import tpu as pltpu

F32 = jnp.float32
BF16 = jnp.bfloat16

D_MODEL = 1024
HEAD_DIM = 64
CONV_CH = 256
CONV_WIDTH = 31
DSA_HEADS = 6
MOBA_HEADS = 6
DSA_W = DSA_HEADS * HEAD_DIM
MOBA_W = MOBA_HEADS * HEAD_DIM
MIX_W = CONV_CH + DSA_W + MOBA_W
IDX_HEADS = 4
IDX_DIM = 32
DSA_TOPK = 256
MOBA_BLOCK = 256
MOBA_TOPK = 3
ROPE_THETA = 500000.0
ROPE_FRACTION = 4
N_EXPERTS = 16
N_EXPERT_GROUPS = 4
EXPERTS_PER_GROUP = N_EXPERTS // N_EXPERT_GROUPS
TOP_K = 2
D_EXPERT = 256
EPS = 1e-6

LANES = 128
NEG = -1e30
VMEM_LIMIT = 56 * 1024 * 1024

_OFF = {}
_o = 0
for _name, _w in (("cu", CONV_CH), ("cg", CONV_CH), ("dq", DSA_W), ("dk", HEAD_DIM), ("dv", HEAD_DIM),
                  ("iq", IDX_HEADS * IDX_DIM), ("ik", IDX_DIM), ("iw", IDX_HEADS),
                  ("mq", MOBA_W), ("mk", MOBA_W), ("mv", MOBA_W)):
    _OFF[_name] = (_o, _o + _w)
    _o += _w

C_CUCG = (0, 512)
C_DQ = (512, 896)
C_IQ = (896, 1024)
C_KK = (1024, 1152)
C_MQ = (1152, 1536)
C_MK = (1536, 1920)
N_PROJ = 1920
N_PROJ_T = 464

NT_DIMS = (((1,), (1,)), ((), ()))


def _nt(a, b, precision=None):
    return lax.dot_general(a, b, NT_DIMS, preferred_element_type=F32, precision=precision)


def _cparams(*sem):
    return pltpu.CompilerParams(dimension_semantics=sem, vmem_limit_bytes=VMEM_LIMIT)


def _ada_kernel(c_ref, w_ref, b_ref, o_ref):
    c = c_ref[...]
    ca = c * jax.nn.sigmoid(c)
    o_ref[...] = jnp.dot(ca, w_ref[...], preferred_element_type=F32,
                         precision=lax.Precision.HIGHEST) + b_ref[...]


def _ada(c, w_ada, b_ada):
    depth, d, n = w_ada.shape
    bsz = c.shape[0]
    tn = 1536
    return pl.pallas_call(
        _ada_kernel,
        grid=(depth, n // tn),
        in_specs=[pl.BlockSpec((bsz, d), lambda l, j: (0, 0)),
                  pl.BlockSpec((None, d, tn), lambda l, j: (l, 0, j)),
                  pl.BlockSpec((None, 1, tn), lambda l, j: (l, 0, j))],
        out_specs=pl.BlockSpec((None, bsz, tn), lambda l, j: (l, 0, j)),
        out_shape=jax.ShapeDtypeStruct((depth, bsz, n), F32),
        name="ada_params",
        compiler_params=_cparams("arbitrary", "arbitrary"),
    )(c, w_ada, b_ada.reshape(depth, 1, n))


def _rms_mod(x, g, shift, scale):
    ms = jnp.mean(x * x, axis=-1, keepdims=True)
    h = x * lax.rsqrt(ms + EPS) * g
    return h * (1.0 + scale) + shift


def _in_kernel(x_ref, mod_ref, g_ref, w_ref, wt_ref, rope_ref,
               cucg_ref, dq_ref, iq_ref, kk_ref, mq_ref, mk_ref, kmean_ref, vt_ref, mvt_ref, iwt_ref):
    tm = x_ref.shape[0]
    h = _rms_mod(x_ref[...], g_ref[...], mod_ref[0:1, :], mod_ref[1:2, :])
    hb = h.astype(BF16)

    def proj(cols):
        return jnp.dot(hb, w_ref[:, cols[0]:cols[1]], preferred_element_type=F32)

    def rope(v, t0, half):
        return (v * rope_ref[t0] + pltpu.roll(v, LANES - half, 1) * rope_ref[t0 + 1]
                + pltpu.roll(v, half, 1) * rope_ref[t0 + 2])

    cucg_ref[...] = proj(C_CUCG).astype(BF16)

    p = proj(C_DQ)
    for ch in range(3):
        sl = slice(ch * LANES, (ch + 1) * LANES)
        dq_ref[:, sl] = rope(p[:, sl], 0, 8).astype(BF16)

    iq_ref[...] = rope(proj(C_IQ), 3, 4).astype(BF16)

    p = proj(C_KK)
    kk = (p * rope_ref[6] + pltpu.roll(p, LANES - 8, 1) * rope_ref[7] + pltpu.roll(p, 8, 1) * rope_ref[8]
          + pltpu.roll(p, LANES - 4, 1) * rope_ref[9] + pltpu.roll(p, 4, 1) * rope_ref[10])
    kk_ref[...] = kk.astype(BF16)

    p = proj(C_MQ)
    for ch in range(3):
        sl = slice(ch * LANES, (ch + 1) * LANES)
        mq_ref[:, sl] = rope(p[:, sl], 0, 8).astype(BF16)

    p = proj(C_MK)
    for ch in range(3):
        sl = slice(ch * LANES, (ch + 1) * LANES)
        r = rope(p[:, sl], 0, 8)
        mk_ref[:, sl] = r.astype(BF16)
        for blk in range(tm // MOBA_BLOCK):
            kmean_ref[blk:blk + 1, sl] = jnp.mean(r[blk * MOBA_BLOCK:(blk + 1) * MOBA_BLOCK], axis=0, keepdims=True)

    pt = _nt(wt_ref[...], hb)
    vt_ref[...] = pt[0:HEAD_DIM].astype(BF16)
    mvt_ref[...] = pt[HEAD_DIM:HEAD_DIM + MOBA_W].astype(BF16)
    iwt_ref[...] = pt[HEAD_DIM + MOBA_W:N_PROJ_T]


def _in_proj(x, mod3, g, w_p, w_t, rope_tabs, tm=512):
    bsz, seq, d = x.shape
    nt = seq // tm
    nb = tm // MOBA_BLOCK
    tok = lambda w: pl.BlockSpec((None, tm, w), lambda b, t: (b, t, 0))
    out_shape = (
        jax.ShapeDtypeStruct((bsz, seq, 512), BF16),
        jax.ShapeDtypeStruct((bsz, seq, DSA_W), BF16),
        jax.ShapeDtypeStruct((bsz, seq, LANES), BF16),
        jax.ShapeDtypeStruct((bsz, seq, LANES), BF16),
        jax.ShapeDtypeStruct((bsz, seq, MOBA_W), BF16),
        jax.ShapeDtypeStruct((bsz, seq, MOBA_W), BF16),
        jax.ShapeDtypeStruct((bsz, nt, nb, MOBA_W), F32),
        jax.ShapeDtypeStruct((bsz, HEAD_DIM, seq), BF16),
        jax.ShapeDtypeStruct((bsz, MOBA_W, seq), BF16),
        jax.ShapeDtypeStruct((bsz, 16, seq), F32),
    )
    out_specs = (
        tok(512), tok(DSA_W), tok(LANES), tok(LANES), tok(MOBA_W), tok(MOBA_W),
        pl.BlockSpec((None, None, nb, MOBA_W), lambda b, t: (b, t, 0, 0)),
        pl.BlockSpec((None, HEAD_DIM, tm), lambda b, t: (b, 0, t)),
        pl.BlockSpec((None, MOBA_W, tm), lambda b, t: (b, 0, t)),
        pl.BlockSpec((None, 16, tm), lambda b, t: (b, 0, t)),
    )
    return pl.pallas_call(
        _in_kernel,
        grid=(bsz, nt),
        in_specs=[tok(d),
                  pl.BlockSpec((None, 6, d), lambda b, t: (b, 0, 0)),
                  pl.BlockSpec((1, d), lambda b, t: (0, 0)),
                  pl.BlockSpec((d, N_PROJ), lambda b, t: (0, 0)),
                  pl.BlockSpec((N_PROJ_T, d), lambda b, t: (0, 0)),
                  pl.BlockSpec((11, tm, LANES), lambda b, t: (0, t, 0))],
        out_specs=out_specs,
        out_shape=out_shape,
        name="in_proj",
        compiler_params=_cparams("parallel", "parallel"),
    )(x, mod3, g, w_p, w_t, rope_tabs)


SUBLANES = 8
CONV_PAD = 32
CONV_TAIL = 16
CONV_ROWS = 128
CONV_FIRST = CONV_PAD - (CONV_WIDTH - 1)
CONV_SPAN = CONV_ROWS + 2 * SUBLANES


def _conv_kernel(cucg_ref, wdw_ref, bdw_ref, lg_ref, lb_ref, y_ref, a_ref):
    seq = cucg_ref.shape[0]
    a_ref[0:CONV_PAD, :] = jnp.zeros((CONV_PAD, CONV_CH), F32)
    a_ref[seq + CONV_PAD:seq + CONV_PAD + CONV_TAIL, :] = jnp.zeros((CONV_TAIL, CONV_CH), F32)

    def glu(c, carry):
        base = pl.multiple_of(c * 256, 256)
        u = cucg_ref[pl.ds(base, 256), 0:CONV_CH].astype(F32)
        g = cucg_ref[pl.ds(base, 256), CONV_CH:2 * CONV_CH].astype(F32)
        a_ref[pl.ds(base + CONV_PAD, 256), :] = u * jax.nn.sigmoid(g)
        return carry

    lax.fori_loop(0, seq // 256, glu, 0)

    def step(c, carry):
        base = pl.multiple_of(c * CONV_ROWS, CONV_ROWS)
        acc = jnp.zeros((CONV_ROWS, CONV_CH), F32) + bdw_ref[...]
        for r in range(SUBLANES):
            part = None
            for j in range(r, CONV_WIDTH, SUBLANES):
                term = a_ref[pl.ds(base + (j - r), CONV_SPAN), :] * wdw_ref[j:j + 1, :]
                part = term if part is None else part + term
            shift = CONV_FIRST + r
            acc = acc + part[shift:shift + CONV_ROWS, :]
        mu = jnp.mean(acc, axis=-1, keepdims=True)
        yc = acc - mu
        var = jnp.mean(yc * yc, axis=-1, keepdims=True)
        yn = yc * lax.rsqrt(var + EPS) * lg_ref[...] + lb_ref[...]
        y_ref[pl.ds(base, CONV_ROWS), :] = (yn * jax.nn.sigmoid(yn)).astype(BF16)
        return carry

    lax.fori_loop(0, seq // CONV_ROWS, step, 0)


def _conv(cucg, w_dw, b_dw, ln_g, ln_b):
    bsz, seq, _ = cucg.shape
    row = pl.BlockSpec((1, CONV_CH), lambda b: (0, 0))
    return pl.pallas_call(
        _conv_kernel,
        grid=(bsz,),
        in_specs=[pl.BlockSpec((None, seq, 2 * CONV_CH), lambda b: (b, 0, 0)),
                  pl.BlockSpec((CONV_WIDTH, CONV_CH), lambda b: (0, 0)),
                  row, row, row],
        out_specs=pl.BlockSpec((None, seq, CONV_CH), lambda b: (b, 0, 0)),
        out_shape=jax.ShapeDtypeStruct((bsz, seq, CONV_CH), BF16),
        scratch_shapes=[pltpu.VMEM((seq + CONV_PAD + CONV_TAIL, CONV_CH), F32)],
        name="conv_group",
        compiler_params=_cparams("parallel"),
    )(cucg, w_dw, b_dw.reshape(1, -1), ln_g.reshape(1, -1), ln_b.reshape(1, -1))


DSA_KC = 256
DSA_TQ = 256
INT_MIN = -2 ** 31


def _key_to_float(k):
    b = jnp.where(k < 0, jnp.int32(INT_MIN) - k, k)
    return lax.bitcast_convert_type(b, F32)


def _dsa_kernel(iq_ref, kk_ref, dq_ref, vt_ref, iwt_ref, y_ref, sc_ref, acc_ref, q_ref, iqw_ref, sa_ref, sb_ref):
    i = pl.program_id(1)
    nch = ((i + 1) * DSA_TQ + DSA_KC - 1) // DSA_KC
    qpos = i * DSA_TQ + lax.broadcasted_iota(jnp.int32, (1, DSA_TQ), 1)
    krow = lax.broadcasted_iota(jnp.int32, (DSA_KC, DSA_TQ), 0)

    for h in range(IDX_HEADS):
        iqw_ref[h * DSA_TQ:(h + 1) * DSA_TQ, :] = iq_ref[:, h * IDX_DIM:(h + 1) * IDX_DIM]
    wwide = jnp.concatenate([iwt_ref[h:h + 1, :] for h in range(IDX_HEADS)], axis=1) * (IDX_HEADS ** -0.5)

    def idx_body(c, carry):
        base = pl.multiple_of(c * DSA_KC, DSA_KC)
        lg = _nt(kk_ref[pl.ds(base, DSA_KC), HEAD_DIM:HEAD_DIM + IDX_DIM], iqw_ref[...])
        r = jnp.maximum(lg * (IDX_DIM ** -0.5), 0.0) * wwide
        sc = r[:, 0:DSA_TQ]
        for h in range(1, IDX_HEADS):
            sc = sc + r[:, h * DSA_TQ:(h + 1) * DSA_TQ]
        sc_ref[pl.ds(base, DSA_KC), :] = jnp.where(krow + base <= qpos, sc, -jnp.inf)
        return carry

    lax.fori_loop(0, nch, idx_body, 0)

    def tree_sum8(hit):
        while hit.shape[0] > 8:
            half = hit.shape[0] // 2
            hit = hit[:half] + hit[half:]
        return hit

    def count(pred):
        def body(c, acc):
            base = pl.multiple_of(c * DSA_KC, DSA_KC)
            return acc + tree_sum8(pred(sc_ref[pl.ds(base, DSA_KC), :], krow + base))
        acc = lax.fori_loop(0, nch, body, jnp.zeros((8, DSA_TQ), F32))
        return jnp.sum(acc, axis=0, keepdims=True)

    topk = float(DSA_TOPK)
    cnt = count(lambda s, k: jnp.where(s >= 0.0, 1.0, 0.0))
    cand0 = jnp.where(cnt >= topk, jnp.int32(0), jnp.int32(INT_MIN))

    def bit_body(step, cand):
        t = cand | (jnp.int32(1) << (30 - step))
        tf = _key_to_float(t)
        cnt = count(lambda s, k: jnp.where(s >= tf, 1.0, 0.0))
        return jnp.where(cnt >= topk, t, cand)

    cand = lax.fori_loop(0, 31, bit_body, cand0)
    thr = jnp.where(cand == jnp.int32(INT_MIN), -jnp.inf, _key_to_float(cand))

    idx_bits = (kk_ref.shape[0] - 1).bit_length()
    need = topk - count(lambda s, k: jnp.where(s > thr, 1.0, 0.0))
    n_tie = count(lambda s, k: jnp.where(s == thr, 1.0, 0.0))

    def tie_body(step, last):
        t = last | (jnp.int32(1) << (idx_bits - 1 - step))
        cnt = count(lambda s, k: jnp.where(s == thr, jnp.where(k < t, 1.0, 0.0), 0.0))
        return jnp.where(cnt < need, t, last)

    def tie_search():
        return lax.fori_loop(0, idx_bits, tie_body, jnp.zeros((1, DSA_TQ), jnp.int32))

    surplus = jnp.max(jnp.where(thr > -jnp.inf, n_tie - need, 0.0))
    last = lax.cond(surplus > 0.0, tie_search,
                    lambda: jnp.full((1, DSA_TQ), 2 ** idx_bits - 1, jnp.int32))

    acc_ref[...] = jnp.zeros(acc_ref.shape, F32)
    for h in range(DSA_HEADS):
        q_ref[h * DSA_TQ:(h + 1) * DSA_TQ, :] = dq_ref[:, h * HEAD_DIM:(h + 1) * HEAD_DIM]
    wide = DSA_HEADS * DSA_TQ
    half_rows = DSA_KC // 2
    ones_rows = jnp.ones((16, half_rows), BF16)
    hrow = lax.broadcasted_iota(jnp.int32, (half_rows, DSA_TQ), 0)

    def masked_scores(c, half):
        base = pl.multiple_of(c * DSA_KC + half * half_rows, half_rows)
        s_idx = sc_ref[pl.ds(base, half_rows), :]
        kidx = hrow + base
        bias = jnp.where(s_idx > thr, 0.0,
                         jnp.where(s_idx == thr, jnp.where(kidx <= last, 0.0, NEG), NEG))
        bias = jnp.where(kidx <= qpos, bias, NEG)
        s = _nt(kk_ref[pl.ds(base, half_rows), 0:HEAD_DIM], q_ref[...])
        return s + jnp.concatenate([bias] * DSA_HEADS, axis=1)

    def softmax_step(s, m_old, l_old, c, half):
        m_new = jnp.maximum(m_old, jnp.max(s, axis=0, keepdims=True))
        alpha = jnp.exp2(m_old - m_new)
        pb = jnp.exp2((s - m_new).astype(BF16))
        l_new = alpha * l_old + jnp.dot(ones_rows, pb, preferred_element_type=F32)[0:1, :]
        vtc = vt_ref[:, pl.ds(pl.multiple_of(c * DSA_KC + half * half_rows, half_rows), half_rows)]
        pv = jnp.concatenate(
            [jnp.dot(vtc, pb[:, h * DSA_TQ:(h + 1) * DSA_TQ], preferred_element_type=F32)
             for h in range(DSA_HEADS)], axis=1)
        acc_ref[...] = alpha * acc_ref[...] + pv
        return m_new, l_new

    sa_ref[...] = masked_scores(0, 0)

    def att_body(c, carry):
        m, l = carry
        sb_ref[...] = masked_scores(c, 1)
        m, l = softmax_step(sa_ref[...], m, l, c, 0)
        sa_ref[...] = masked_scores(jnp.minimum(c + 1, nch - 1), 0)
        return softmax_step(sb_ref[...], m, l, c, 1)

    _, l_fin = lax.fori_loop(0, nch, att_body,
                             (jnp.full((1, wide), NEG, F32), jnp.zeros((1, wide), F32)))

    out = acc_ref[...] / l_fin
    for pair in range(DSA_HEADS // 2):
        two = jnp.concatenate([out[:, (2 * pair) * DSA_TQ:(2 * pair + 1) * DSA_TQ],
                               out[:, (2 * pair + 1) * DSA_TQ:(2 * pair + 2) * DSA_TQ]], axis=0)
        y_ref[:, pair * LANES:(pair + 1) * LANES] = two.T.astype(BF16)


def _dsa(iq, kk, dq, vt, iwt):
    bsz, seq, _ = dq.shape
    nq = seq // DSA_TQ
    qblk = lambda w: pl.BlockSpec((None, DSA_TQ, w), lambda b, i: (b, i, 0))
    return pl.pallas_call(
        _dsa_kernel,
        grid=(bsz, nq),
        in_specs=[qblk(LANES),
                  pl.BlockSpec((None, seq, LANES), lambda b, i: (b, 0, 0)),
                  qblk(DSA_W),
                  pl.BlockSpec((None, HEAD_DIM, seq), lambda b, i: (b, 0, 0)),
                  pl.BlockSpec((None, 16, DSA_TQ), lambda b, i: (b, 0, i))],
        out_specs=qblk(DSA_W),
        out_shape=jax.ShapeDtypeStruct((bsz, seq, DSA_W), BF16),
        scratch_shapes=[pltpu.VMEM((seq, DSA_TQ), F32),
                        pltpu.VMEM((HEAD_DIM, DSA_HEADS * DSA_TQ), F32),
                        pltpu.VMEM((DSA_HEADS * DSA_TQ, HEAD_DIM), BF16),
                        pltpu.VMEM((IDX_HEADS * DSA_TQ, IDX_DIM), BF16),
                        pltpu.VMEM((DSA_KC // 2, DSA_HEADS * DSA_TQ), F32),
                        pltpu.VMEM((DSA_KC // 2, DSA_HEADS * DSA_TQ), F32)],
        name="dsa_group",
        compiler_params=_cparams("parallel", "parallel"),
    )(iq, kk, dq, vt, iwt)


def _moba_kernel(mq_ref, mk_ref, mvt_ref, kmean_ref, y_ref, bias_ref, o_ref, sa_ref, sb_ref):
    own = pl.program_id(1)
    blk = MOBA_BLOCK
    nb = kmean_ref.shape[0]
    heads = range(MOBA_HEADS)
    cols = [slice(h * HEAD_DIM, (h + 1) * HEAD_DIM) for h in heads]
    lanes = [slice(h * blk, (h + 1) * blk) for h in heads]
    wide = MOBA_HEADS * blk

    half_rows = blk // 2

    def scores(j, half):
        base = pl.multiple_of(j * blk + half * half_rows, half_rows)
        return jnp.concatenate(
            [_nt(mk_ref[pl.ds(base, half_rows), cols[h]], mq_ref[:, cols[h]]) for h in heads], axis=1)

    ones_rows = jnp.ones((16, half_rows), BF16)

    def softmax_step(s, m_old, l_old, j, half):
        base = pl.multiple_of(j * blk + half * half_rows, half_rows)
        m_new = jnp.maximum(m_old, jnp.max(s, axis=0, keepdims=True))
        alpha = jnp.exp2(m_old - m_new)
        pb = jnp.exp2((s - m_new).astype(BF16))
        l_new = alpha * l_old + jnp.dot(ones_rows, pb, preferred_element_type=F32)[0:1, :]
        pv = jnp.concatenate(
            [jnp.dot(mvt_ref[cols[h], pl.ds(base, half_rows)], pb[:, lanes[h]], preferred_element_type=F32)
             for h in heads], axis=1)
        o_ref[...] = alpha * o_ref[...] + pv
        return m_new, l_new

    km = kmean_ref[...]
    k_hi = km.astype(BF16).astype(F32)
    k_mid = (km - k_hi).astype(BF16).astype(F32)
    k_lo = (km - k_hi - k_mid).astype(BF16).astype(F32)
    pieces = jnp.concatenate([k_hi, k_mid, k_lo, jnp.zeros_like(km)], axis=0).astype(BF16)
    g3 = jnp.concatenate([_nt(pieces[:, cols[h]], mq_ref[:, cols[h]]) for h in heads], axis=1)
    gt = g3[0:nb] + g3[nb:2 * nb] + g3[2 * nb:3 * nb]
    ridx = lax.broadcasted_iota(jnp.int32, (nb, wide), 0)
    for j in range(nb):
        gj = gt[j:j + 1, :]
        beats = jnp.where(gt > gj, 1.0, jnp.where(gt == gj, jnp.where(ridx < j, 1.0, 0.0), 0.0))
        beats = jnp.where(ridx < own, beats, 0.0)
        rank = jnp.sum(beats, axis=0, keepdims=True)
        bias_ref[j:j + 1, :] = jnp.where(rank < float(MOBA_TOPK), 0.0, NEG)

    o_ref[...] = jnp.zeros(o_ref.shape, F32)
    carry = (jnp.full((1, wide), NEG, F32), jnp.zeros((1, wide), F32))
    for half in range(2):
        krow = lax.broadcasted_iota(jnp.int32, (half_rows, blk), 0) + half * half_rows
        causal = jnp.where(krow <= lax.broadcasted_iota(jnp.int32, (half_rows, blk), 1), 0.0, NEG)
        s = scores(own, half) + jnp.concatenate([causal] * MOBA_HEADS, axis=1)
        carry = softmax_step(s, *carry, own, half)

    sa_ref[...] = scores(0, 0) + bias_ref[0:1, :]

    def past(j, carry):
        m, l = carry
        sb_ref[...] = scores(j, 1) + bias_ref[pl.ds(j, 1), :]
        m, l = softmax_step(sa_ref[...], m, l, j, 0)
        nxt = jnp.minimum(j + 1, jnp.maximum(own - 1, 0))
        sa_ref[...] = scores(nxt, 0) + bias_ref[pl.ds(nxt, 1), :]
        return softmax_step(sb_ref[...], m, l, j, 1)

    _, l_fin = lax.fori_loop(0, own, past, carry)

    out = o_ref[...] / l_fin
    for pair in range(MOBA_HEADS // 2):
        two = jnp.concatenate([out[:, lanes[2 * pair]], out[:, lanes[2 * pair + 1]]], axis=0)
        y_ref[:, pair * LANES:(pair + 1) * LANES] = two.T.astype(BF16)


def _moba(mq, mk, mvt, kmean):
    bsz, seq, _ = mq.shape
    nb = seq // MOBA_BLOCK
    qblk = pl.BlockSpec((None, MOBA_BLOCK, MOBA_W), lambda b, i: (b, i, 0))
    return pl.pallas_call(
        _moba_kernel,
        grid=(bsz, nb),
        in_specs=[qblk,
                  pl.BlockSpec((None, seq, MOBA_W), lambda b, i: (b, 0, 0)),
                  pl.BlockSpec((None, MOBA_W, seq), lambda b, i: (b, 0, 0)),
                  pl.BlockSpec((None, nb, MOBA_W), lambda b, i: (b, 0, 0))],
        out_specs=qblk,
        out_shape=jax.ShapeDtypeStruct((bsz, seq, MOBA_W), BF16),
        scratch_shapes=[pltpu.VMEM((nb, MOBA_HEADS * MOBA_BLOCK), F32),
                        pltpu.VMEM((HEAD_DIM, MOBA_HEADS * MOBA_BLOCK), F32),
                        pltpu.VMEM((MOBA_BLOCK // 2, MOBA_HEADS * MOBA_BLOCK), F32),
                        pltpu.VMEM((MOBA_BLOCK // 2, MOBA_HEADS * MOBA_BLOCK), F32)],
        name="moba_group",
        compiler_params=_cparams("parallel", "parallel"),
    )(mq, mk, mvt, kmean)


def _pair_max(vals):
    best = None
    for a in range(len(vals)):
        for b in range(a + 1, len(vals)):
            s = vals[a] + vals[b]
            best = s if best is None else jnp.maximum(best, s)
    return best


def _tail_kernel(x_ref, yc_ref, yd_ref, ym_ref, mod_ref, gffn_ref, wo_ref, wrt_ref, br_ref,
                 wg_ref, wu_ref, wd_ref, gfin_ref, o_ref, gates_ref, *, final_norm):
    tm = x_ref.shape[0]
    mix = (jnp.dot(yc_ref[...], wo_ref[0:CONV_CH, :], preferred_element_type=F32)
           + jnp.dot(yd_ref[...], wo_ref[CONV_CH:CONV_CH + DSA_W, :], preferred_element_type=F32)
           + jnp.dot(ym_ref[...], wo_ref[CONV_CH + DSA_W:MIX_W, :], preferred_element_type=F32))
    x1 = x_ref[...] + mod_ref[2:3, :] * mix
    h = _rms_mod(x1, gffn_ref[...], mod_ref[3:4, :], mod_ref[4:5, :])
    hb = h.astype(BF16)

    aff_t = jax.nn.sigmoid(_nt(wrt_ref[...], h, precision=lax.Precision.HIGHEST))
    bia_t = aff_t + br_ref[...]
    aff = [aff_t[e:e + 1, :] for e in range(N_EXPERTS)]
    bia = [bia_t[e:e + 1, :] for e in range(N_EXPERTS)]
    gscore = [_pair_max(bia[g * EXPERTS_PER_GROUP:(g + 1) * EXPERTS_PER_GROUP]) for g in range(N_EXPERT_GROUPS)]
    best = jnp.zeros((1, tm), jnp.int32)
    best_s = gscore[0]
    for g in range(1, N_EXPERT_GROUPS):
        better = gscore[g] > best_s
        best = jnp.where(better, g, best)
        best_s = jnp.where(better, gscore[g], best_s)
    wsel = []
    for e in range(N_EXPERTS):
        g = e // EXPERTS_PER_GROUP
        rank = jnp.zeros((1, tm), F32)
        for e2 in range(g * EXPERTS_PER_GROUP, (g + 1) * EXPERTS_PER_GROUP):
            if e2 < e:
                rank = rank + jnp.where(bia[e2] >= bia[e], 1.0, 0.0)
            elif e2 > e:
                rank = rank + jnp.where(bia[e2] > bia[e], 1.0, 0.0)
        wsel.append(jnp.where(best == g, jnp.where(rank < float(TOP_K), aff[e], 0.0), 0.0))
    wsum = wsel[0]
    for e in range(1, N_EXPERTS):
        wsum = wsum + wsel[e]
    gates_t = jnp.concatenate([w / wsum for w in wsel]
                              + [jnp.zeros((LANES - N_EXPERTS, tm), F32)], axis=0)
    gates_ref[...] = gates_t.T

    acc = jnp.zeros((tm, D_MODEL), F32)
    for e in range(N_EXPERTS):
        ge = jnp.dot(hb, wg_ref[e], preferred_element_type=F32)
        ue = jnp.dot(hb, wu_ref[e], preferred_element_type=F32)
        a = ge * jax.nn.sigmoid(ge) * ue
        a = (a * gates_ref[:, e:e + 1]).astype(BF16)
        acc = acc + jnp.dot(a, wd_ref[e], preferred_element_type=F32)

    x2 = x1 + mod_ref[5:6, :] * acc
    if final_norm:
        ms = jnp.mean(x2 * x2, axis=-1, keepdims=True)
        x2 = x2 * lax.rsqrt(ms + EPS) * gfin_ref[...]
    o_ref[...] = x2


def _tail(x2d, yc, yd, ym, mod3, g_ffn, w_o, w_rt, b_r, wg, wu, wd, g_fin, seq, final_norm, tm=512):
    ntok, d = x2d.shape
    per_b = seq // tm
    tok = lambda w: pl.BlockSpec((tm, w), lambda t: (t, 0))
    const = lambda shape: pl.BlockSpec(shape, lambda t: (0,) * len(shape), pipeline_mode=pl.Buffered(1))
    return pl.pallas_call(
        functools.partial(_tail_kernel, final_norm=final_norm),
        grid=(ntok // tm,),
        in_specs=[tok(d), tok(CONV_CH), tok(DSA_W), tok(MOBA_W),
                  pl.BlockSpec((None, 6, d), lambda t: (t // per_b, 0, 0)),
                  const((1, d)),
                  const((MIX_W, d)),
                  const((N_EXPERTS, d)),
                  const((N_EXPERTS, 1)),
                  const((N_EXPERTS, d, D_EXPERT)),
                  const((N_EXPERTS, d, D_EXPERT)),
                  const((N_EXPERTS, D_EXPERT, d)),
                  const((1, d))],
        out_specs=tok(d),
        out_shape=jax.ShapeDtypeStruct((ntok, d), F32),
        scratch_shapes=[pltpu.VMEM((tm, LANES), F32)],
        name="tail_moe",
        compiler_params=_cparams("parallel"),
    )(x2d, yc, yd, ym, mod3, g_ffn, w_o, w_rt, b_r, wg, wu, wd, g_fin)


def _rope_tables(seq):
    pos = jnp.arange(seq).astype(F32)[:, None]

    def inv_freq(d):
        rot = d // ROPE_FRACTION
        return jnp.exp(-math.log(ROPE_THETA) * (jnp.arange(rot // 2, dtype=F32) * 2.0 / rot))

    inv_all = jnp.concatenate([inv_freq(HEAD_DIM), inv_freq(IDX_DIM), jnp.zeros((1,), F32)])
    n64 = HEAD_DIM // ROPE_FRACTION // 2
    n32 = IDX_DIM // ROPE_FRACTION // 2
    none = n64 + n32

    def lane_plan(widths):
        idx, first, second, width = [], [], [], []
        for d in widths:
            half = d // ROPE_FRACTION // 2
            off = 0 if d == HEAD_DIM else n64
            for i in range(d):
                idx.append(off + i % half if i < 2 * half else none)
                first.append(1.0 if i < half else 0.0)
                second.append(1.0 if half <= i < 2 * half else 0.0)
                width.append(d)
        pad = LANES - len(idx)
        return (np.array(idx + [none] * pad), np.array(first + [0.0] * pad, np.float32),
                np.array(second + [0.0] * pad, np.float32), np.array(width + [0] * pad))

    def tables(widths, split_by_width=False):
        idx, first, second, width = lane_plan(widths)
        ang = pos * inv_all[idx][None, :]
        cos, sin = jnp.cos(ang), jnp.sin(ang)
        if not split_by_width:
            return [cos, -sin * first[None, :], sin * second[None, :]]
        out = [cos]
        for d in sorted(set(widths), reverse=True):
            sel = (width == d).astype(np.float32)
            out += [-sin * (first * sel)[None, :], sin * (second * sel)[None, :]]
        return out

    return jnp.stack(tables([HEAD_DIM] * (LANES // HEAD_DIM)) + tables([IDX_DIM] * (LANES // IDX_DIM))
                     + tables([HEAD_DIM, IDX_DIM], split_by_width=True), axis=0)


def _prep_in_weights(w):
    d = w.shape[0]
    col = lambda name: w[:, _OFF[name][0]:_OFF[name][1]]
    scale = HEAD_DIM ** -0.5 * math.log2(math.e)
    w_p = jnp.concatenate([
        col("cu"), col("cg"), col("dq") * scale, col("iq"),
        col("dk"), col("ik"), jnp.zeros((d, LANES - HEAD_DIM - IDX_DIM), F32),
        col("mq") * scale, col("mk")], axis=1).astype(BF16)
    w_t = jnp.concatenate([
        col("dv"), col("mv"), col("iw"), jnp.zeros((d, N_PROJ_T - HEAD_DIM - MOBA_W - IDX_HEADS), F32)],
        axis=1).T.astype(BF16)
    return w_p, w_t


def kernel(x, c, w_ada, b_ada, g_mix, w_in, w_dw, b_dw, ln_conv_g, ln_conv_b, w_o, g_ffn, w_router, b_router, w_gate, w_up, w_down, g_final):
    bsz, seq, d = x.shape
    depth = w_ada.shape[0]
    mod = _ada(c, w_ada, b_ada).reshape(depth, bsz, 6, d)
    rope_tabs = _rope_tables(seq)
    w_rt = w_router.T
    b_r = b_router.reshape(N_EXPERTS, 1)
    g_fin = g_final.reshape(1, d)
    for l in range(depth):
        w_p, w_t = _prep_in_weights(w_in[l])
        cucg, dq, iq, kk, mq, mk, kmean, vt, mvt, iwt = _in_proj(
            x, mod[l], g_mix[l].reshape(1, d), w_p, w_t, rope_tabs)
        y_conv = _conv(cucg, w_dw[l], b_dw[l], ln_conv_g[l], ln_conv_b[l])
        y_dsa = _dsa(iq, kk, dq, vt, iwt)
        y_moba = _moba(mq, mk, mvt, kmean.reshape(bsz, seq // MOBA_BLOCK, MOBA_W))
        x = _tail(x.reshape(bsz * seq, d), y_conv.reshape(bsz * seq, CONV_CH),
                  y_dsa.reshape(bsz * seq, DSA_W), y_moba.reshape(bsz * seq, MOBA_W),
                  mod[l], g_ffn[l].reshape(1, d), w_o[l].astype(BF16), w_rt, b_r,
                  w_gate[l].astype(BF16), w_up[l].astype(BF16), w_down[l].astype(BF16), g_fin,
                  seq, final_norm=(l == depth - 1)).reshape(bsz, seq, d)
    return x
```

```python
import functools
import math

import jax
import jax.numpy as jnp
import numpy as np
from jax import lax
from jax.experimental import pallas as pl
from jax.experimental.pallas import tpu as pltpu

F32 = jnp.float32
BF16 = jnp.bfloat16

D_MODEL = 1024
HEAD_DIM = 64
CONV_CH = 256
CONV_WIDTH = 31
DSA_HEADS = 6
MOBA_HEADS = 6
DSA_W = DSA_HEADS * HEAD_DIM
MOBA_W = MOBA_HEADS * HEAD_DIM
MIX_W = CONV_CH + DSA_W + MOBA_W
IDX_HEADS = 4
IDX_DIM = 32
DSA_TOPK = 256
MOBA_BLOCK = 256
MOBA_TOPK = 3
ROPE_THETA = 500000.0
ROPE_FRACTION = 4
N_EXPERTS = 16
N_EXPERT_GROUPS = 4
EXPERTS_PER_GROUP = N_EXPERTS // N_EXPERT_GROUPS
TOP_K = 2
D_EXPERT = 256
EPS = 1e-6

LANES = 128
NEG = -1e30
VMEM_LIMIT = 56 * 1024 * 1024

_OFF = {}
_o = 0
for _name, _w in (("cu", CONV_CH), ("cg", CONV_CH), ("dq", DSA_W), ("dk", HEAD_DIM), ("dv", HEAD_DIM),
                  ("iq", IDX_HEADS * IDX_DIM), ("ik", IDX_DIM), ("iw", IDX_HEADS),
                  ("mq", MOBA_W), ("mk", MOBA_W), ("mv", MOBA_W)):
    _OFF[_name] = (_o, _o + _w)
    _o += _w

C_CUCG = (0, 512)
C_DQ = (512, 896)
C_IQ = (896, 1024)
C_KK = (1024, 1152)
C_MQ = (1152, 1536)
C_MK = (1536, 1920)
N_PROJ = 1920
N_PROJ_T = 464

NT_DIMS = (((1,), (1,)), ((), ()))


def _nt(a, b, precision=None):
    return lax.dot_general(a, b, NT_DIMS, preferred_element_type=F32, precision=precision)


def _cparams(*sem):
    return pltpu.CompilerParams(dimension_semantics=sem, vmem_limit_bytes=VMEM_LIMIT)


def _ada_kernel(c_ref, w_ref, b_ref, o_ref):
    c = c_ref[...]
    ca = c * jax.nn.sigmoid(c)
    o_ref[...] = jnp.dot(ca, w_ref[...], preferred_element_type=F32,
                         precision=lax.Precision.HIGHEST) + b_ref[...]


def _ada(c, w_ada, b_ada):
    depth, d, n = w_ada.shape
    bsz = c.shape[0]
    tn = 1536
    return pl.pallas_call(
        _ada_kernel,
        grid=(depth, n // tn),
        in_specs=[pl.BlockSpec((bsz, d), lambda l, j: (0, 0)),
                  pl.BlockSpec((None, d, tn), lambda l, j: (l, 0, j)),
                  pl.BlockSpec((None, 1, tn), lambda l, j: (l, 0, j))],
        out_specs=pl.BlockSpec((None, bsz, tn), lambda l, j: (l, 0, j)),
        out_shape=jax.ShapeDtypeStruct((depth, bsz, n), F32),
        name="ada_params",
        compiler_params=_cparams("arbitrary", "arbitrary"),
    )(c, w_ada, b_ada.reshape(depth, 1, n))


def _rms_mod(x, g, shift, scale):
    ms = jnp.mean(x * x, axis=-1, keepdims=True)
    h = x * lax.rsqrt(ms + EPS) * g
    return h * (1.0 + scale) + shift


def _in_kernel(x_ref, mod_ref, g_ref, w_ref, wt_ref, rope_ref,
               cucg_ref, dq_ref, iq_ref, kk_ref, mq_ref, mk_ref, kmean_ref, vt_ref, mvt_ref, iwt_ref):
    tm = x_ref.shape[0]
    h = _rms_mod(x_ref[...], g_ref[...], mod_ref[0:1, :], mod_ref[1:2, :])
    hb = h.astype(BF16)

    def proj(cols):
        return jnp.dot(hb, w_ref[:, cols[0]:cols[1]], preferred_element_type=F32)

    def rope(v, t0, half):
        return (v * rope_ref[t0] + pltpu.roll(v, LANES - half, 1) * rope_ref[t0 + 1]
                + pltpu.roll(v, half, 1) * rope_ref[t0 + 2])

    cucg_ref[...] = proj(C_CUCG).astype(BF16)

    p = proj(C_DQ)
    for ch in range(3):
        sl = slice(ch * LANES, (ch + 1) * LANES)
        dq_ref[:, sl] = rope(p[:, sl], 0, 8).astype(BF16)

    iq_ref[...] = rope(proj(C_IQ), 3, 4).astype(BF16)

    p = proj(C_KK)
    kk = (p * rope_ref[6] + pltpu.roll(p, LANES - 8, 1) * rope_ref[7] + pltpu.roll(p, 8, 1) * rope_ref[8]
          + pltpu.roll(p, LANES - 4, 1) * rope_ref[9] + pltpu.roll(p, 4, 1) * rope_ref[10])
    kk_ref[...] = kk.astype(BF16)

    p = proj(C_MQ)
    for ch in range(3):
        sl = slice(ch * LANES, (ch + 1) * LANES)
        mq_ref[:, sl] = rope(p[:, sl], 0, 8).astype(BF16)

    p = proj(C_MK)
    for ch in range(3):
        sl = slice(ch * LANES, (ch + 1) * LANES)
        r = rope(p[:, sl], 0, 8)
        mk_ref[:, sl] = r.astype(BF16)
        for blk in range(tm // MOBA_BLOCK):
            kmean_ref[blk:blk + 1, sl] = jnp.mean(r[blk * MOBA_BLOCK:(blk + 1) * MOBA_BLOCK], axis=0, keepdims=True)

    pt = _nt(wt_ref[...], hb)
    vt_ref[...] = pt[0:HEAD_DIM].astype(BF16)
    mvt_ref[...] = pt[HEAD_DIM:HEAD_DIM + MOBA_W].astype(BF16)
    iwt_ref[...] = pt[HEAD_DIM + MOBA_W:N_PROJ_T]


def _in_proj(x, mod3, g, w_p, w_t, rope_tabs, tm=512):
    bsz, seq, d = x.shape
    nt = seq // tm
    nb = tm // MOBA_BLOCK
    tok = lambda w: pl.BlockSpec((None, tm, w), lambda b, t: (b, t, 0))
    out_shape = (
        jax.ShapeDtypeStruct((bsz, seq, 512), BF16),
        jax.ShapeDtypeStruct((bsz, seq, DSA_W), BF16),
        jax.ShapeDtypeStruct((bsz, seq, LANES), BF16),
        jax.ShapeDtypeStruct((bsz, seq, LANES), BF16),
        jax.ShapeDtypeStruct((bsz, seq, MOBA_W), BF16),
        jax.ShapeDtypeStruct((bsz, seq, MOBA_W), BF16),
        jax.ShapeDtypeStruct((bsz, nt, nb, MOBA_W), F32),
        jax.ShapeDtypeStruct((bsz, HEAD_DIM, seq), BF16),
        jax.ShapeDtypeStruct((bsz, MOBA_W, seq), BF16),
        jax.ShapeDtypeStruct((bsz, 16, seq), F32),
    )
    out_specs = (
        tok(512), tok(DSA_W), tok(LANES), tok(LANES), tok(MOBA_W), tok(MOBA_W),
        pl.BlockSpec((None, None, nb, MOBA_W), lambda b, t: (b, t, 0, 0)),
        pl.BlockSpec((None, HEAD_DIM, tm), lambda b, t: (b, 0, t)),
        pl.BlockSpec((None, MOBA_W, tm), lambda b, t: (b, 0, t)),
        pl.BlockSpec((None, 16, tm), lambda b, t: (b, 0, t)),
    )
    return pl.pallas_call(
        _in_kernel,
        grid=(bsz, nt),
        in_specs=[tok(d),
                  pl.BlockSpec((None, 6, d), lambda b, t: (b, 0, 0)),
                  pl.BlockSpec((1, d), lambda b, t: (0, 0)),
                  pl.BlockSpec((d, N_PROJ), lambda b, t: (0, 0)),
                  pl.BlockSpec((N_PROJ_T, d), lambda b, t: (0, 0)),
                  pl.BlockSpec((11, tm, LANES), lambda b, t: (0, t, 0))],
        out_specs=out_specs,
        out_shape=out_shape,
        name="in_proj",
        compiler_params=_cparams("parallel", "parallel"),
    )(x, mod3, g, w_p, w_t, rope_tabs)


SUBLANES = 8
CONV_PAD = 32
CONV_TAIL = 16
CONV_ROWS = 128
CONV_FIRST = CONV_PAD - (CONV_WIDTH - 1)
CONV_SPAN = CONV_ROWS + 2 * SUBLANES


def _conv_kernel(cucg_ref, wdw_ref, bdw_ref, lg_ref, lb_ref, y_ref, a_ref):
    seq = cucg_ref.shape[0]
    a_ref[0:CONV_PAD, :] = jnp.zeros((CONV_PAD, CONV_CH), F32)
    a_ref[seq + CONV_PAD:seq + CONV_PAD + CONV_TAIL, :] = jnp.zeros((CONV_TAIL, CONV_CH), F32)

    def glu(c, carry):
        base = pl.multiple_of(c * 256, 256)
        u = cucg_ref[pl.ds(base, 256), 0:CONV_CH].astype(F32)
        g = cucg_ref[pl.ds(base, 256), CONV_CH:2 * CONV_CH].astype(F32)
        a_ref[pl.ds(base + CONV_PAD, 256), :] = u * jax.nn.sigmoid(g)
        return carry

    lax.fori_loop(0, seq // 256, glu, 0)

    def step(c, carry):
        base = pl.multiple_of(c * CONV_ROWS, CONV_ROWS)
        acc = jnp.zeros((CONV_ROWS, CONV_CH), F32) + bdw_ref[...]
        for r in range(SUBLANES):
            part = None
            for j in range(r, CONV_WIDTH, SUBLANES):
                term = a_ref[pl.ds(base + (j - r), CONV_SPAN), :] * wdw_ref[j:j + 1, :]
                part = term if part is None else part + term
            shift = CONV_FIRST + r
            acc = acc + part[shift:shift + CONV_ROWS, :]
        mu = jnp.mean(acc, axis=-1, keepdims=True)
        yc = acc - mu
        var = jnp.mean(yc * yc, axis=-1, keepdims=True)
        yn = yc * lax.rsqrt(var + EPS) * lg_ref[...] + lb_ref[...]
        y_ref[pl.ds(base, CONV_ROWS), :] = (yn * jax.nn.sigmoid(yn)).astype(BF16)
        return carry

    lax.fori_loop(0, seq // CONV_ROWS, step, 0)


def _conv(cucg, w_dw, b_dw, ln_g, ln_b):
    bsz, seq, _ = cucg.shape
    row = pl.BlockSpec((1, CONV_CH), lambda b: (0, 0))
    return pl.pallas_call(
        _conv_kernel,
        grid=(bsz,),
        in_specs=[pl.BlockSpec((None, seq, 2 * CONV_CH), lambda b: (b, 0, 0)),
                  pl.BlockSpec((CONV_WIDTH, CONV_CH), lambda b: (0, 0)),
                  row, row, row],
        out_specs=pl.BlockSpec((None, seq, CONV_CH), lambda b: (b, 0, 0)),
        out_shape=jax.ShapeDtypeStruct((bsz, seq, CONV_CH), BF16),
        scratch_shapes=[pltpu.VMEM((seq + CONV_PAD + CONV_TAIL, CONV_CH), F32)],
        name="conv_group",
        compiler_params=_cparams("parallel"),
    )(cucg, w_dw, b_dw.reshape(1, -1), ln_g.reshape(1, -1), ln_b.reshape(1, -1))


DSA_KC = 256
DSA_TQ = 256
INT_MIN = -2 ** 31


def _key_to_float(k):
    b = jnp.where(k < 0, jnp.int32(INT_MIN) - k, k)
    return lax.bitcast_convert_type(b, F32)


def _dsa_kernel(iq_ref, kk_ref, dq_ref, vt_ref, iwt_ref, y_ref, sc_ref, acc_ref, q_ref, iqw_ref, sa_ref, sb_ref, thr_ref, last_ref):
    i = pl.program_id(1)
    nch = ((i + 1) * DSA_TQ + DSA_KC - 1) // DSA_KC
    qpos = i * DSA_TQ + lax.broadcasted_iota(jnp.int32, (1, DSA_TQ), 1)
    krow = lax.broadcasted_iota(jnp.int32, (DSA_KC, DSA_TQ), 0)

    for h in range(IDX_HEADS):
        iqw_ref[h * DSA_TQ:(h + 1) * DSA_TQ, :] = iq_ref[:, h * IDX_DIM:(h + 1) * IDX_DIM]
    wwide = jnp.concatenate([iwt_ref[h:h + 1, :] for h in range(IDX_HEADS)], axis=1) * (IDX_HEADS ** -0.5)

    def idx_body(c, carry):
        base = pl.multiple_of(c * DSA_KC, DSA_KC)
        lg = _nt(kk_ref[pl.ds(base, DSA_KC), HEAD_DIM:HEAD_DIM + IDX_DIM], iqw_ref[...])
        r = jnp.maximum(lg * (IDX_DIM ** -0.5), 0.0) * wwide
        sc = r[:, 0:DSA_TQ]
        for h in range(1, IDX_HEADS):
            sc = sc + r[:, h * DSA_TQ:(h + 1) * DSA_TQ]
        sc_ref[pl.ds(base, DSA_KC), :] = jnp.where(krow + base <= qpos, sc, -jnp.inf)
        return carry

    lax.fori_loop(0, nch, idx_body, 0)

    topk = float(DSA_TOPK)
    idx_bits = (kk_ref.shape[0] - 1).bit_length()
    take_all_ties = jnp.full((1, DSA_TQ), 2 ** idx_bits - 1, jnp.int32)
    row8 = lax.broadcasted_iota(jnp.int32, (SUBLANES, DSA_TQ), 0)

    def select(n):
        if n * DSA_KC <= DSA_TOPK:
            return jnp.full((1, DSA_TQ), -jnp.inf, F32), take_all_ties

        def counts(preds):
            lanes_acc = 2
            accs = [[None] * lanes_acc for _ in preds]
            for r in range(n * DSA_KC // SUBLANES):
                s = sc_ref[r * SUBLANES:(r + 1) * SUBLANES, :]
                for k, pred in enumerate(preds):
                    hit = pred(s, row8 + r * SUBLANES)
                    prev = accs[k][r % lanes_acc]
                    accs[k][r % lanes_acc] = hit if prev is None else prev + hit
            return [jnp.sum(sum(a[1:], a[0]), axis=0, keepdims=True) for a in accs]

        def at_least(tf):
            return lambda s, k: jnp.where(s >= tf, 1.0, 0.0)

        (cnt,) = counts([at_least(0.0)])
        cand = jnp.where(cnt >= topk, jnp.int32(0), jnp.int32(INT_MIN))

        def one_bit(step, cand):
            t = cand | (jnp.int32(1) << (30 - step))
            (cnt,) = counts([at_least(_key_to_float(t))])
            return jnp.where(cnt >= topk, t, cand)

        cand = lax.fori_loop(0, 31, one_bit, cand)
        thr = jnp.where(cand == jnp.int32(INT_MIN), -jnp.inf, _key_to_float(cand))

        n_above, n_tie = counts([lambda s, k: jnp.where(s > thr, 1.0, 0.0),
                                 lambda s, k: jnp.where(s == thr, 1.0, 0.0)])
        need = topk - n_above

        def tie_body(step, last):
            t = last | (jnp.int32(1) << (idx_bits - 1 - step))
            (cnt,) = counts([lambda s, k: jnp.where(s == thr, jnp.where(k < t, 1.0, 0.0), 0.0)])
            return jnp.where(cnt < need, t, last)

        def tie_search():
            return lax.fori_loop(0, idx_bits, tie_body, jnp.zeros((1, DSA_TQ), jnp.int32))

        surplus = jnp.max(jnp.where(thr > -jnp.inf, n_tie - need, 0.0))
        return thr, lax.cond(surplus > 0.0, tie_search, lambda: take_all_ties)

    for n in range(1, kk_ref.shape[0] // DSA_KC + 1):
        @pl.when(nch == n)
        def _():
            thr_n, last_n = select(n)
            thr_ref[0:1, :] = thr_n
            last_ref[0:1, :] = last_n

    thr = thr_ref[0:1, :]
    last = last_ref[0:1, :]

    acc_ref[...] = jnp.zeros(acc_ref.shape, F32)
    for h in range(DSA_HEADS):
        q_ref[h * DSA_TQ:(h + 1) * DSA_TQ, :] = dq_ref[:, h * HEAD_DIM:(h + 1) * HEAD_DIM]
    wide = DSA_HEADS * DSA_TQ
    half_rows = DSA_KC // 2
    ones_rows = jnp.ones((16, half_rows), BF16)
    hrow = lax.broadcasted_iota(jnp.int32, (half_rows, DSA_TQ), 0)

    def masked_scores(c, half):
        base = pl.multiple_of(c * DSA_KC + half * half_rows, half_rows)
        s_idx = sc_ref[pl.ds(base, half_rows), :]
        kidx = hrow + base
        bias = jnp.where(s_idx > thr, 0.0,
                         jnp.where(s_idx == thr, jnp.where(kidx <= last, 0.0, NEG), NEG))
        bias = jnp.where(kidx <= qpos, bias, NEG)
        s = _nt(kk_ref[pl.ds(base, half_rows), 0:HEAD_DIM], q_ref[...])
        return s + jnp.concatenate([bias] * DSA_HEADS, axis=1)

    def softmax_step(s, m_old, l_old, c, half):
        m_new = jnp.maximum(m_old, jnp.max(s, axis=0, keepdims=True))
        alpha = jnp.exp2(m_old - m_new)
        pb = jnp.exp2((s - m_new).astype(BF16))
        l_new = alpha * l_old + jnp.dot(ones_rows, pb, preferred_element_type=F32)[0:1, :]
        vtc = vt_ref[:, pl.ds(pl.multiple_of(c * DSA_KC + half * half_rows, half_rows), half_rows)]
        pv = jnp.concatenate(
            [jnp.dot(vtc, pb[:, h * DSA_TQ:(h + 1) * DSA_TQ], preferred_element_type=F32)
             for h in range(DSA_HEADS)], axis=1)
        acc_ref[...] = alpha * acc_ref[...] + pv
        return m_new, l_new

    sa_ref[...] = masked_scores(0, 0)

    def att_body(c, carry):
        m, l = carry
        sb_ref[...] = masked_scores(c, 1)
        m, l = softmax_step(sa_ref[...], m, l, c, 0)
        sa_ref[...] = masked_scores(jnp.minimum(c + 1, nch - 1), 0)
        return softmax_step(sb_ref[...], m, l, c, 1)

    _, l_fin = lax.fori_loop(0, nch, att_body,
                             (jnp.full((1, wide), NEG, F32), jnp.zeros((1, wide), F32)))

    out = acc_ref[...] / l_fin
    for pair in range(DSA_HEADS // 2):
        two = jnp.concatenate([out[:, (2 * pair) * DSA_TQ:(2 * pair + 1) * DSA_TQ],
                               out[:, (2 * pair + 1) * DSA_TQ:(2 * pair + 2) * DSA_TQ]], axis=0)
        y_ref[:, pair * LANES:(pair + 1) * LANES] = two.T.astype(BF16)


def _dsa(iq, kk, dq, vt, iwt):
    bsz, seq, _ = dq.shape
    nq = seq // DSA_TQ
    qblk = lambda w: pl.BlockSpec((None, DSA_TQ, w), lambda b, i: (b, i, 0))
    return pl.pallas_call(
        _dsa_kernel,
        grid=(bsz, nq),
        in_specs=[qblk(LANES),
                  pl.BlockSpec((None, seq, LANES), lambda b, i: (b, 0, 0)),
                  qblk(DSA_W),
                  pl.BlockSpec((None, HEAD_DIM, seq), lambda b, i: (b, 0, 0)),
                  pl.BlockSpec((None, 16, DSA_TQ), lambda b, i: (b, 0, i))],
        out_specs=qblk(DSA_W),
        out_shape=jax.ShapeDtypeStruct((bsz, seq, DSA_W), BF16),
        scratch_shapes=[pltpu.VMEM((seq, DSA_TQ), F32),
                        pltpu.VMEM((HEAD_DIM, DSA_HEADS * DSA_TQ), F32),
                        pltpu.VMEM((DSA_HEADS * DSA_TQ, HEAD_DIM), BF16),
                        pltpu.VMEM((IDX_HEADS * DSA_TQ, IDX_DIM), BF16),
                        pltpu.VMEM((DSA_KC // 2, DSA_HEADS * DSA_TQ), F32),
                        pltpu.VMEM((DSA_KC // 2, DSA_HEADS * DSA_TQ), F32),
                        pltpu.VMEM((8, DSA_TQ), F32),
                        pltpu.VMEM((8, DSA_TQ), jnp.int32)],
        name="dsa_group",
        compiler_params=_cparams("parallel", "parallel"),
    )(iq, kk, dq, vt, iwt)


def _moba_kernel(mq_ref, mk_ref, mvt_ref, kmean_ref, y_ref, bias_ref, o_ref, sa_ref, sb_ref):
    own = pl.program_id(1)
    blk = MOBA_BLOCK
    nb = kmean_ref.shape[0]
    heads = range(MOBA_HEADS)
    cols = [slice(h * HEAD_DIM, (h + 1) * HEAD_DIM) for h in heads]
    lanes = [slice(h * blk, (h + 1) * blk) for h in heads]
    wide = MOBA_HEADS * blk

    half_rows = blk // 2

    def scores(j, half):
        base = pl.multiple_of(j * blk + half * half_rows, half_rows)
        return jnp.concatenate(
            [_nt(mk_ref[pl.ds(base, half_rows), cols[h]], mq_ref[:, cols[h]]) for h in heads], axis=1)

    ones_rows = jnp.ones((16, half_rows), BF16)

    def softmax_step(s, m_old, l_old, j, half):
        base = pl.multiple_of(j * blk + half * half_rows, half_rows)
        m_new = jnp.maximum(m_old, jnp.max(s, axis=0, keepdims=True))
        alpha = jnp.exp2(m_old - m_new)
        pb = jnp.exp2((s - m_new).astype(BF16))
        l_new = alpha * l_old + jnp.dot(ones_rows, pb, preferred_element_type=F32)[0:1, :]
        pv = jnp.concatenate(
            [jnp.dot(mvt_ref[cols[h], pl.ds(base, half_rows)], pb[:, lanes[h]], preferred_element_type=F32)
             for h in heads], axis=1)
        o_ref[...] = alpha * o_ref[...] + pv
        return m_new, l_new

    km = kmean_ref[...]
    k_hi = km.astype(BF16).astype(F32)
    k_mid = (km - k_hi).astype(BF16).astype(F32)
    k_lo = (km - k_hi - k_mid).astype(BF16).astype(F32)
    pieces = jnp.concatenate([k_hi, k_mid, k_lo, jnp.zeros_like(km)], axis=0).astype(BF16)
    g3 = jnp.concatenate([_nt(pieces[:, cols[h]], mq_ref[:, cols[h]]) for h in heads], axis=1)
    gt = g3[0:nb] + g3[nb:2 * nb] + g3[2 * nb:3 * nb]
    ridx = lax.broadcasted_iota(jnp.int32, (nb, wide), 0)
    for j in range(nb):
        gj = gt[j:j + 1, :]
        beats = jnp.where(gt > gj, 1.0, jnp.where(gt == gj, jnp.where(ridx < j, 1.0, 0.0), 0.0))
        beats = jnp.where(ridx < own, beats, 0.0)
        rank = jnp.sum(beats, axis=0, keepdims=True)
        bias_ref[j:j + 1, :] = jnp.where(rank < float(MOBA_TOPK), 0.0, NEG)

    o_ref[...] = jnp.zeros(o_ref.shape, F32)
    carry = (jnp.full((1, wide), NEG, F32), jnp.zeros((1, wide), F32))
    for half in range(2):
        krow = lax.broadcasted_iota(jnp.int32, (half_rows, blk), 0) + half * half_rows
        causal = jnp.where(krow <= lax.broadcasted_iota(jnp.int32, (half_rows, blk), 1), 0.0, NEG)
        s = scores(own, half) + jnp.concatenate([causal] * MOBA_HEADS, axis=1)
        carry = softmax_step(s, *carry, own, half)

    sa_ref[...] = scores(0, 0) + bias_ref[0:1, :]

    def past(j, carry):
        m, l = carry
        sb_ref[...] = scores(j, 1) + bias_ref[pl.ds(j, 1), :]
        m, l = softmax_step(sa_ref[...], m, l, j, 0)
        nxt = jnp.minimum(j + 1, jnp.maximum(own - 1, 0))
        sa_ref[...] = scores(nxt, 0) + bias_ref[pl.ds(nxt, 1), :]
        return softmax_step(sb_ref[...], m, l, j, 1)

    _, l_fin = lax.fori_loop(0, own, past, carry)

    out = o_ref[...] / l_fin
    for pair in range(MOBA_HEADS // 2):
        two = jnp.concatenate([out[:, lanes[2 * pair]], out[:, lanes[2 * pair + 1]]], axis=0)
        y_ref[:, pair * LANES:(pair + 1) * LANES] = two.T.astype(BF16)


def _moba(mq, mk, mvt, kmean):
    bsz, seq, _ = mq.shape
    nb = seq // MOBA_BLOCK
    qblk = pl.BlockSpec((None, MOBA_BLOCK, MOBA_W), lambda b, i: (b, i, 0))
    return pl.pallas_call(
        _moba_kernel,
        grid=(bsz, nb),
        in_specs=[qblk,
                  pl.BlockSpec((None, seq, MOBA_W), lambda b, i: (b, 0, 0)),
                  pl.BlockSpec((None, MOBA_W, seq), lambda b, i: (b, 0, 0)),
                  pl.BlockSpec((None, nb, MOBA_W), lambda b, i: (b, 0, 0))],
        out_specs=qblk,
        out_shape=jax.ShapeDtypeStruct((bsz, seq, MOBA_W), BF16),
        scratch_shapes=[pltpu.VMEM((nb, MOBA_HEADS * MOBA_BLOCK), F32),
                        pltpu.VMEM((HEAD_DIM, MOBA_HEADS * MOBA_BLOCK), F32),
                        pltpu.VMEM((MOBA_BLOCK // 2, MOBA_HEADS * MOBA_BLOCK), F32),
                        pltpu.VMEM((MOBA_BLOCK // 2, MOBA_HEADS * MOBA_BLOCK), F32)],
        name="moba_group",
        compiler_params=_cparams("parallel", "parallel"),
    )(mq, mk, mvt, kmean)


def _pair_max(vals):
    best = None
    for a in range(len(vals)):
        for b in range(a + 1, len(vals)):
            s = vals[a] + vals[b]
            best = s if best is None else jnp.maximum(best, s)
    return best


def _tail_kernel(x_ref, yc_ref, yd_ref, ym_ref, mod_ref, gffn_ref, wo_ref, wrt_ref, br_ref,
                 wg_ref, wu_ref, wd_ref, gfin_ref, o_ref, gates_ref, *, final_norm):
    tm = x_ref.shape[0]
    mix = (jnp.dot(yc_ref[...], wo_ref[0:CONV_CH, :], preferred_element_type=F32)
           + jnp.dot(yd_ref[...], wo_ref[CONV_CH:CONV_CH + DSA_W, :], preferred_element_type=F32)
           + jnp.dot(ym_ref[...], wo_ref[CONV_CH + DSA_W:MIX_W, :], preferred_element_type=F32))
    x1 = x_ref[...] + mod_ref[2:3, :] * mix
    h = _rms_mod(x1, gffn_ref[...], mod_ref[3:4, :], mod_ref[4:5, :])
    hb = h.astype(BF16)

    aff_t = jax.nn.sigmoid(_nt(wrt_ref[...], h, precision=lax.Precision.HIGHEST))
    bia_t = aff_t + br_ref[...]
    aff = [aff_t[e:e + 1, :] for e in range(N_EXPERTS)]
    bia = [bia_t[e:e + 1, :] for e in range(N_EXPERTS)]
    gscore = [_pair_max(bia[g * EXPERTS_PER_GROUP:(g + 1) * EXPERTS_PER_GROUP]) for g in range(N_EXPERT_GROUPS)]
    best = jnp.zeros((1, tm), jnp.int32)
    best_s = gscore[0]
    for g in range(1, N_EXPERT_GROUPS):
        better = gscore[g] > best_s
        best = jnp.where(better, g, best)
        best_s = jnp.where(better, gscore[g], best_s)
    wsel = []
    for e in range(N_EXPERTS):
        g = e // EXPERTS_PER_GROUP
        rank = jnp.zeros((1, tm), F32)
        for e2 in range(g * EXPERTS_PER_GROUP, (g + 1) * EXPERTS_PER_GROUP):
            if e2 < e:
                rank = rank + jnp.where(bia[e2] >= bia[e], 1.0, 0.0)
            elif e2 > e:
                rank = rank + jnp.where(bia[e2] > bia[e], 1.0, 0.0)
        wsel.append(jnp.where(best == g, jnp.where(rank < float(TOP_K), aff[e], 0.0), 0.0))
    wsum = wsel[0]
    for e in range(1, N_EXPERTS):
        wsum = wsum + wsel[e]
    gates_t = jnp.concatenate([w / wsum for w in wsel]
                              + [jnp.zeros((LANES - N_EXPERTS, tm), F32)], axis=0)
    gates_ref[...] = gates_t.T

    acc = jnp.zeros((tm, D_MODEL), F32)
    for e in range(N_EXPERTS):
        ge = jnp.dot(hb, wg_ref[e], preferred_element_type=F32)
        ue = jnp.dot(hb, wu_ref[e], preferred_element_type=F32)
        a = ge * jax.nn.sigmoid(ge) * ue
        a = (a * gates_ref[:, e:e + 1]).astype(BF16)
        acc = acc + jnp.dot(a, wd_ref[e], preferred_element_type=F32)

    x2 = x1 + mod_ref[5:6, :] * acc
    if final_norm:
        ms = jnp.mean(x2 * x2, axis=-1, keepdims=True)
        x2 = x2 * lax.rsqrt(ms + EPS) * gfin_ref[...]
    o_ref[...] = x2


def _tail(x2d, yc, yd, ym, mod3, g_ffn, w_o, w_rt, b_r, wg, wu, wd, g_fin, seq, final_norm, tm=512):
    ntok, d = x2d.shape
    per_b = seq // tm
    tok = lambda w: pl.BlockSpec((tm, w), lambda t: (t, 0))
    const = lambda shape: pl.BlockSpec(shape, lambda t: (0,) * len(shape), pipeline_mode=pl.Buffered(1))
    return pl.pallas_call(
        functools.partial(_tail_kernel, final_norm=final_norm),
        grid=(ntok // tm,),
        in_specs=[tok(d), tok(CONV_CH), tok(DSA_W), tok(MOBA_W),
                  pl.BlockSpec((None, 6, d), lambda t: (t // per_b, 0, 0)),
                  const((1, d)),
                  const((MIX_W, d)),
                  const((N_EXPERTS, d)),
                  const((N_EXPERTS, 1)),
                  const((N_EXPERTS, d, D_EXPERT)),
                  const((N_EXPERTS, d, D_EXPERT)),
                  const((N_EXPERTS, D_EXPERT, d)),
                  const((1, d))],
        out_specs=tok(d),
        out_shape=jax.ShapeDtypeStruct((ntok, d), F32),
        scratch_shapes=[pltpu.VMEM((tm, LANES), F32)],
        name="tail_moe",
        compiler_params=_cparams("parallel"),
    )(x2d, yc, yd, ym, mod3, g_ffn, w_o, w_rt, b_r, wg, wu, wd, g_fin)


def _rope_tables(seq):
    pos = jnp.arange(seq).astype(F32)[:, None]

    def inv_freq(d):
        rot = d // ROPE_FRACTION
        return jnp.exp(-math.log(ROPE_THETA) * (jnp.arange(rot // 2, dtype=F32) * 2.0 / rot))

    inv_all = jnp.concatenate([inv_freq(HEAD_DIM), inv_freq(IDX_DIM), jnp.zeros((1,), F32)])
    n64 = HEAD_DIM // ROPE_FRACTION // 2
    n32 = IDX_DIM // ROPE_FRACTION // 2
    none = n64 + n32

    def lane_plan(widths):
        idx, first, second, width = [], [], [], []
        for d in widths:
            half = d // ROPE_FRACTION // 2
            off = 0 if d == HEAD_DIM else n64
            for i in range(d):
                idx.append(off + i % half if i < 2 * half else none)
                first.append(1.0 if i < half else 0.0)
                second.append(1.0 if half <= i < 2 * half else 0.0)
                width.append(d)
        pad = LANES - len(idx)
        return (np.array(idx + [none] * pad), np.array(first + [0.0] * pad, np.float32),
                np.array(second + [0.0] * pad, np.float32), np.array(width + [0] * pad))

    def tables(widths, split_by_width=False):
        idx, first, second, width = lane_plan(widths)
        ang = pos * inv_all[idx][None, :]
        cos, sin = jnp.cos(ang), jnp.sin(ang)
        if not split_by_width:
            return [cos, -sin * first[None, :], sin * second[None, :]]
        out = [cos]
        for d in sorted(set(widths), reverse=True):
            sel = (width == d).astype(np.float32)
            out += [-sin * (first * sel)[None, :], sin * (second * sel)[None, :]]
        return out

    return jnp.stack(tables([HEAD_DIM] * (LANES // HEAD_DIM)) + tables([IDX_DIM] * (LANES // IDX_DIM))
                     + tables([HEAD_DIM, IDX_DIM], split_by_width=True), axis=0)


def _prep_in_weights(w):
    d = w.shape[0]
    col = lambda name: w[:, _OFF[name][0]:_OFF[name][1]]
    scale = HEAD_DIM ** -0.5 * math.log2(math.e)
    w_p = jnp.concatenate([
        col("cu"), col("cg"), col("dq") * scale, col("iq"),
        col("dk"), col("ik"), jnp.zeros((d, LANES - HEAD_DIM - IDX_DIM), F32),
        col("mq") * scale, col("mk")], axis=1).astype(BF16)
    w_t = jnp.concatenate([
        col("dv"), col("mv"), col("iw"), jnp.zeros((d, N_PROJ_T - HEAD_DIM - MOBA_W - IDX_HEADS), F32)],
        axis=1).T.astype(BF16)
    return w_p, w_t


def kernel(x, c, w_ada, b_ada, g_mix, w_in, w_dw, b_dw, ln_conv_g, ln_conv_b, w_o, g_ffn, w_router, b_router, w_gate, w_up, w_down, g_final):
    bsz, seq, d = x.shape
    depth = w_ada.shape[0]
    mod = _ada(c, w_ada, b_ada).reshape(depth, bsz, 6, d)
    rope_tabs = _rope_tables(seq)
    w_rt = w_router.T
    b_r = b_router.reshape(N_EXPERTS, 1)
    g_fin = g_final.reshape(1, d)
    for l in range(depth):
        w_p, w_t = _prep_in_weights(w_in[l])
        cucg, dq, iq, kk, mq, mk, kmean, vt, mvt, iwt = _in_proj(
            x, mod[l], g_mix[l].reshape(1, d), w_p, w_t, rope_tabs)
        y_conv = _conv(cucg, w_dw[l], b_dw[l], ln_conv_g[l], ln_conv_b[l])
        y_dsa = _dsa(iq, kk, dq, vt, iwt)
        y_moba = _moba(mq, mk, mvt, kmean.reshape(bsz, seq // MOBA_BLOCK, MOBA_W))
        x = _tail(x.reshape(bsz * seq, d), y_conv.reshape(bsz * seq, CONV_CH),
                  y_dsa.reshape(bsz * seq, DSA_W), y_moba.reshape(bsz * seq, MOBA_W),
                  mod[l], g_ffn[l].reshape(1, d), w_o[l].astype(BF16), w_rt, b_r,
                  w_gate[l].astype(BF16), w_up[l].astype(BF16), w_down[l].astype(BF16), g_fin,
                  seq, final_norm=(l == depth - 1)).reshape(bsz, seq, d)
    return x
```

```python
import functools
import math

import jax
import jax.numpy as jnp
import numpy as np
from jax import lax
from jax.experimental import pallas as pl
from jax.experimental.pallas import tpu as pltpu

F32 = jnp.float32
BF16 = jnp.bfloat16

D_MODEL = 1024
HEAD_DIM = 64
CONV_CH = 256
CONV_WIDTH = 31
DSA_HEADS = 6
MOBA_HEADS = 6
DSA_W = DSA_HEADS * HEAD_DIM
MOBA_W = MOBA_HEADS * HEAD_DIM
MIX_W = CONV_CH + DSA_W + MOBA_W
IDX_HEADS = 4
IDX_DIM = 32
DSA_TOPK = 256
MOBA_BLOCK = 256
MOBA_TOPK = 3
ROPE_THETA = 500000.0
ROPE_FRACTION = 4
N_EXPERTS = 16
N_EXPERT_GROUPS = 4
EXPERTS_PER_GROUP = N_EXPERTS // N_EXPERT_GROUPS
TOP_K = 2
D_EXPERT = 256
EPS = 1e-6

LANES = 128
NEG = -1e30
VMEM_LIMIT = 56 * 1024 * 1024

_OFF = {}
_o = 0
for _name, _w in (("cu", CONV_CH), ("cg", CONV_CH), ("dq", DSA_W), ("dk", HEAD_DIM), ("dv", HEAD_DIM),
                  ("iq", IDX_HEADS * IDX_DIM), ("ik", IDX_DIM), ("iw", IDX_HEADS),
                  ("mq", MOBA_W), ("mk", MOBA_W), ("mv", MOBA_W)):
    _OFF[_name] = (_o, _o + _w)
    _o += _w

C_CUCG = (0, 512)
C_DQ = (512, 896)
C_IQ = (896, 1024)
C_KK = (1024, 1152)
C_MQ = (1152, 1536)
C_MK = (1536, 1920)
N_PROJ = 1920
N_PROJ_T = 464

NT_DIMS = (((1,), (1,)), ((), ()))


def _nt(a, b, precision=None):
    return lax.dot_general(a, b, NT_DIMS, preferred_element_type=F32, precision=precision)


def _cparams(*sem):
    return pltpu.CompilerParams(dimension_semantics=sem, vmem_limit_bytes=VMEM_LIMIT)


def _ada_kernel(c_ref, w_ref, b_ref, o_ref):
    c = c_ref[...]
    ca = c * jax.nn.sigmoid(c)
    o_ref[...] = jnp.dot(ca, w_ref[...], preferred_element_type=F32,
                         precision=lax.Precision.HIGHEST) + b_ref[...]


def _ada(c, w_ada, b_ada):
    depth, d, n = w_ada.shape
    bsz = c.shape[0]
    tn = 1536
    return pl.pallas_call(
        _ada_kernel,
        grid=(depth, n // tn),
        in_specs=[pl.BlockSpec((bsz, d), lambda l, j: (0, 0)),
                  pl.BlockSpec((None, d, tn), lambda l, j: (l, 0, j)),
                  pl.BlockSpec((None, 1, tn), lambda l, j: (l, 0, j))],
        out_specs=pl.BlockSpec((None, bsz, tn), lambda l, j: (l, 0, j)),
        out_shape=jax.ShapeDtypeStruct((depth, bsz, n), F32),
        name="ada_params",
        compiler_params=_cparams("arbitrary", "arbitrary"),
    )(c, w_ada, b_ada.reshape(depth, 1, n))


def _rms_mod(x, g, shift, scale):
    ms = jnp.mean(x * x, axis=-1, keepdims=True)
    h = x * lax.rsqrt(ms + EPS) * g
    return h * (1.0 + scale) + shift


def _in_kernel(x_ref, mod_ref, g_ref, w_ref, wt_ref, rope_ref,
               cucg_ref, dq_ref, iq_ref, kk_ref, mq_ref, mk_ref, kmean_ref, vt_ref, mvt_ref, iwt_ref):
    tm = x_ref.shape[0]
    h = _rms_mod(x_ref[...], g_ref[...], mod_ref[0:1, :], mod_ref[1:2, :])
    hb = h.astype(BF16)

    def proj(cols):
        return jnp.dot(hb, w_ref[:, cols[0]:cols[1]], preferred_element_type=F32)

    def rope(v, t0, half):
        return (v * rope_ref[t0] + pltpu.roll(v, LANES - half, 1) * rope_ref[t0 + 1]
                + pltpu.roll(v, half, 1) * rope_ref[t0 + 2])

    cucg_ref[...] = proj(C_CUCG).astype(BF16)

    p = proj(C_DQ)
    for ch in range(3):
        sl = slice(ch * LANES, (ch + 1) * LANES)
        dq_ref[:, sl] = rope(p[:, sl], 0, 8).astype(BF16)

    iq_ref[...] = rope(proj(C_IQ), 3, 4).astype(BF16)

    p = proj(C_KK)
    kk = (p * rope_ref[6] + pltpu.roll(p, LANES - 8, 1) * rope_ref[7] + pltpu.roll(p, 8, 1) * rope_ref[8]
          + pltpu.roll(p, LANES - 4, 1) * rope_ref[9] + pltpu.roll(p, 4, 1) * rope_ref[10])
    kk_ref[...] = kk.astype(BF16)

    p = proj(C_MQ)
    for ch in range(3):
        sl = slice(ch * LANES, (ch + 1) * LANES)
        mq_ref[:, sl] = rope(p[:, sl], 0, 8).astype(BF16)

    p = proj(C_MK)
    for ch in range(3):
        sl = slice(ch * LANES, (ch + 1) * LANES)
        r = rope(p[:, sl], 0, 8)
        mk_ref[:, sl] = r.astype(BF16)
        for blk in range(tm // MOBA_BLOCK):
            kmean_ref[blk:blk + 1, sl] = jnp.mean(r[blk * MOBA_BLOCK:(blk + 1) * MOBA_BLOCK], axis=0, keepdims=True)

    pt = _nt(wt_ref[...], hb)
    vt_ref[...] = pt[0:HEAD_DIM].astype(BF16)
    mvt_ref[...] = pt[HEAD_DIM:HEAD_DIM + MOBA_W].astype(BF16)
    iwt_ref[...] = pt[HEAD_DIM + MOBA_W:N_PROJ_T]


def _in_proj(x, mod3, g, w_p, w_t, rope_tabs, tm=512):
    bsz, seq, d = x.shape
    nt = seq // tm
    nb = tm // MOBA_BLOCK
    tok = lambda w: pl.BlockSpec((None, tm, w), lambda b, t: (b, t, 0))
    out_shape = (
        jax.ShapeDtypeStruct((bsz, seq, 512), BF16),
        jax.ShapeDtypeStruct((bsz, seq, DSA_W), BF16),
        jax.ShapeDtypeStruct((bsz, seq, LANES), BF16),
        jax.ShapeDtypeStruct((bsz, seq, LANES), BF16),
        jax.ShapeDtypeStruct((bsz, seq, MOBA_W), BF16),
        jax.ShapeDtypeStruct((bsz, seq, MOBA_W), BF16),
        jax.ShapeDtypeStruct((bsz, nt, nb, MOBA_W), F32),
        jax.ShapeDtypeStruct((bsz, HEAD_DIM, seq), BF16),
        jax.ShapeDtypeStruct((bsz, MOBA_W, seq), BF16),
        jax.ShapeDtypeStruct((bsz, 16, seq), F32),
    )
    out_specs = (
        tok(512), tok(DSA_W), tok(LANES), tok(LANES), tok(MOBA_W), tok(MOBA_W),
        pl.BlockSpec((None, None, nb, MOBA_W), lambda b, t: (b, t, 0, 0)),
        pl.BlockSpec((None, HEAD_DIM, tm), lambda b, t: (b, 0, t)),
        pl.BlockSpec((None, MOBA_W, tm), lambda b, t: (b, 0, t)),
        pl.BlockSpec((None, 16, tm), lambda b, t: (b, 0, t)),
    )
    return pl.pallas_call(
        _in_kernel,
        grid=(bsz, nt),
        in_specs=[tok(d),
                  pl.BlockSpec((None, 6, d), lambda b, t: (b, 0, 0)),
                  pl.BlockSpec((1, d), lambda b, t: (0, 0)),
                  pl.BlockSpec((d, N_PROJ), lambda b, t: (0, 0)),
                  pl.BlockSpec((N_PROJ_T, d), lambda b, t: (0, 0)),
                  pl.BlockSpec((11, tm, LANES), lambda b, t: (0, t, 0))],
        out_specs=out_specs,
        out_shape=out_shape,
        name="in_proj",
        compiler_params=_cparams("parallel", "parallel"),
    )(x, mod3, g, w_p, w_t, rope_tabs)


SUBLANES = 8
CONV_PAD = 32
CONV_TAIL = 16
CONV_ROWS = 128
CONV_FIRST = CONV_PAD - (CONV_WIDTH - 1)
CONV_SPAN = CONV_ROWS + 2 * SUBLANES


def _conv_kernel(cucg_ref, wdw_ref, bdw_ref, lg_ref, lb_ref, y_ref, a_ref):
    seq = cucg_ref.shape[0]
    a_ref[0:CONV_PAD, :] = jnp.zeros((CONV_PAD, CONV_CH), F32)
    a_ref[seq + CONV_PAD:seq + CONV_PAD + CONV_TAIL, :] = jnp.zeros((CONV_TAIL, CONV_CH), F32)

    def glu(c, carry):
        base = pl.multiple_of(c * 256, 256)
        u = cucg_ref[pl.ds(base, 256), 0:CONV_CH].astype(F32)
        g = cucg_ref[pl.ds(base, 256), CONV_CH:2 * CONV_CH].astype(F32)
        a_ref[pl.ds(base + CONV_PAD, 256), :] = u * jax.nn.sigmoid(g)
        return carry

    lax.fori_loop(0, seq // 256, glu, 0)

    def step(c, carry):
        base = pl.multiple_of(c * CONV_ROWS, CONV_ROWS)
        acc = jnp.zeros((CONV_ROWS, CONV_CH), F32) + bdw_ref[...]
        for r in range(SUBLANES):
            part = None
            for j in range(r, CONV_WIDTH, SUBLANES):
                term = a_ref[pl.ds(base + (j - r), CONV_SPAN), :] * wdw_ref[j:j + 1, :]
                part = term if part is None else part + term
            shift = CONV_FIRST + r
            acc = acc + part[shift:shift + CONV_ROWS, :]
        mu = jnp.mean(acc, axis=-1, keepdims=True)
        yc = acc - mu
        var = jnp.mean(yc * yc, axis=-1, keepdims=True)
        yn = yc * lax.rsqrt(var + EPS) * lg_ref[...] + lb_ref[...]
        y_ref[pl.ds(base, CONV_ROWS), :] = (yn * jax.nn.sigmoid(yn)).astype(BF16)
        return carry

    lax.fori_loop(0, seq // CONV_ROWS, step, 0)


def _conv(cucg, w_dw, b_dw, ln_g, ln_b):
    bsz, seq, _ = cucg.shape
    row = pl.BlockSpec((1, CONV_CH), lambda b: (0, 0))
    return pl.pallas_call(
        _conv_kernel,
        grid=(bsz,),
        in_specs=[pl.BlockSpec((None, seq, 2 * CONV_CH), lambda b: (b, 0, 0)),
                  pl.BlockSpec((CONV_WIDTH, CONV_CH), lambda b: (0, 0)),
                  row, row, row],
        out_specs=pl.BlockSpec((None, seq, CONV_CH), lambda b: (b, 0, 0)),
        out_shape=jax.ShapeDtypeStruct((bsz, seq, CONV_CH), BF16),
        scratch_shapes=[pltpu.VMEM((seq + CONV_PAD + CONV_TAIL, CONV_CH), F32)],
        name="conv_group",
        compiler_params=_cparams("parallel"),
    )(cucg, w_dw, b_dw.reshape(1, -1), ln_g.reshape(1, -1), ln_b.reshape(1, -1))


DSA_KC = 256
DSA_TQ = 256
INT_MIN = -2 ** 31


def _key_to_float(k):
    b = jnp.where(k < 0, jnp.int32(INT_MIN) - k, k)
    return lax.bitcast_convert_type(b, F32)


def _dsa_kernel(iq_ref, kk_ref, dq_ref, vt_ref, iwt_ref, y_ref, sc_ref, acc_ref, q_ref, iqw_ref, qk_ref, thr_ref, last_ref):
    i = pl.program_id(1)
    nch = ((i + 1) * DSA_TQ + DSA_KC - 1) // DSA_KC
    qpos = i * DSA_TQ + lax.broadcasted_iota(jnp.int32, (1, DSA_TQ), 1)
    krow = lax.broadcasted_iota(jnp.int32, (DSA_KC, DSA_TQ), 0)

    for h in range(IDX_HEADS):
        iqw_ref[h * DSA_TQ:(h + 1) * DSA_TQ, :] = iq_ref[:, h * IDX_DIM:(h + 1) * IDX_DIM]
    wwide = jnp.concatenate([iwt_ref[h:h + 1, :] for h in range(IDX_HEADS)], axis=1) * (IDX_HEADS ** -0.5)

    def idx_body(c, carry):
        base = pl.multiple_of(c * DSA_KC, DSA_KC)
        lg = _nt(kk_ref[pl.ds(base, DSA_KC), HEAD_DIM:HEAD_DIM + IDX_DIM], iqw_ref[...])
        r = jnp.maximum(lg * (IDX_DIM ** -0.5), 0.0) * wwide
        sc = r[:, 0:DSA_TQ]
        for h in range(1, IDX_HEADS):
            sc = sc + r[:, h * DSA_TQ:(h + 1) * DSA_TQ]
        sc_ref[pl.ds(base, DSA_KC), :] = jnp.where(krow + base <= qpos, sc, -jnp.inf)
        return carry

    lax.fori_loop(0, nch, idx_body, 0)

    for h in range(DSA_HEADS):
        q_ref[h * DSA_TQ:(h + 1) * DSA_TQ, :] = dq_ref[:, h * HEAD_DIM:(h + 1) * HEAD_DIM]

    def qk_chunk(c):
        rows = slice(c * DSA_KC, (c + 1) * DSA_KC)
        qk_ref[rows, :] = _nt(kk_ref[rows, 0:HEAD_DIM], q_ref[...])

    topk = float(DSA_TOPK)
    idx_bits = (kk_ref.shape[0] - 1).bit_length()
    take_all_ties = jnp.full((1, DSA_TQ), 2 ** idx_bits - 1, jnp.int32)
    row8 = lax.broadcasted_iota(jnp.int32, (SUBLANES, DSA_TQ), 0)

    def select(n):
        if n * DSA_KC <= DSA_TOPK:
            for c in range(n):
                qk_chunk(c)
            return jnp.full((1, DSA_TQ), -jnp.inf, F32), take_all_ties

        def counts(preds):
            lanes_acc = 2
            accs = [[None] * lanes_acc for _ in preds]
            for r in range(n * DSA_KC // SUBLANES):
                s = sc_ref[r * SUBLANES:(r + 1) * SUBLANES, :]
                for k, pred in enumerate(preds):
                    hit = pred(s, row8 + r * SUBLANES)
                    prev = accs[k][r % lanes_acc]
                    accs[k][r % lanes_acc] = hit if prev is None else prev + hit
            return [jnp.sum(sum(a[1:], a[0]), axis=0, keepdims=True) for a in accs]

        def at_least(tf):
            return lambda s, k: jnp.where(s >= tf, 1.0, 0.0)

        qk_chunk(0)
        (cnt,) = counts([at_least(0.0)])
        cand = jnp.where(cnt >= topk, jnp.int32(0), jnp.int32(INT_MIN))

        def one_bit(step, cand):
            t = cand | (jnp.int32(1) << (30 - step))
            (cnt,) = counts([at_least(_key_to_float(t))])
            return jnp.where(cnt >= topk, t, cand)

        for c in range(1, n):
            qk_chunk(c)
            cand = one_bit(c - 1, cand)
        cand = lax.fori_loop(n - 1, 31, one_bit, cand)
        thr = jnp.where(cand == jnp.int32(INT_MIN), -jnp.inf, _key_to_float(cand))

        n_above, n_tie = counts([lambda s, k: jnp.where(s > thr, 1.0, 0.0),
                                 lambda s, k: jnp.where(s == thr, 1.0, 0.0)])
        need = topk - n_above

        def tie_body(step, last):
            t = last | (jnp.int32(1) << (idx_bits - 1 - step))
            (cnt,) = counts([lambda s, k: jnp.where(s == thr, jnp.where(k < t, 1.0, 0.0), 0.0)])
            return jnp.where(cnt < need, t, last)

        def tie_search():
            return lax.fori_loop(0, idx_bits, tie_body, jnp.zeros((1, DSA_TQ), jnp.int32))

        surplus = jnp.max(jnp.where(thr > -jnp.inf, n_tie - need, 0.0))
        return thr, lax.cond(surplus > 0.0, tie_search, lambda: take_all_ties)

    for n in range(1, kk_ref.shape[0] // DSA_KC + 1):
        @pl.when(nch == n)
        def _():
            thr_n, last_n = select(n)
            thr_ref[0:1, :] = thr_n
            last_ref[0:1, :] = last_n

    thr = thr_ref[0:1, :]
    last = last_ref[0:1, :]

    acc_ref[...] = jnp.zeros(acc_ref.shape, F32)
    wide = DSA_HEADS * DSA_TQ
    half_rows = DSA_KC
    ones_rows = jnp.ones((16, half_rows), BF16)
    hrow = lax.broadcasted_iota(jnp.int32, (half_rows, DSA_TQ), 0)

    def softmax_step(base, m_old, l_old):
        s_idx = sc_ref[pl.ds(base, half_rows), :]
        kidx = hrow + base
        bias = jnp.where(s_idx > thr, 0.0,
                         jnp.where(s_idx == thr, jnp.where(kidx <= last, 0.0, NEG), NEG))
        bias = jnp.where(kidx <= qpos, bias, NEG)
        s = qk_ref[pl.ds(base, half_rows), :] + jnp.concatenate([bias] * DSA_HEADS, axis=1)
        m_new = jnp.maximum(m_old, jnp.max(s, axis=0, keepdims=True))
        alpha = jnp.exp2(m_old - m_new)
        pb = jnp.exp2((s - m_new).astype(BF16))
        l_new = alpha * l_old + jnp.dot(ones_rows, pb, preferred_element_type=F32)[0:1, :]
        vtc = vt_ref[:, pl.ds(base, half_rows)]
        pv = jnp.concatenate(
            [jnp.dot(vtc, pb[:, h * DSA_TQ:(h + 1) * DSA_TQ], preferred_element_type=F32)
             for h in range(DSA_HEADS)], axis=1)
        acc_ref[...] = alpha * acc_ref[...] + pv
        return m_new, l_new

    def att_body(c, carry):
        return softmax_step(pl.multiple_of(c * DSA_KC, DSA_KC), *carry)

    _, l_fin = lax.fori_loop(0, nch, att_body,
                             (jnp.full((1, wide), NEG, F32), jnp.zeros((1, wide), F32)))

    out = acc_ref[...] / l_fin
    for pair in range(DSA_HEADS // 2):
        two = jnp.concatenate([out[:, (2 * pair) * DSA_TQ:(2 * pair + 1) * DSA_TQ],
                               out[:, (2 * pair + 1) * DSA_TQ:(2 * pair + 2) * DSA_TQ]], axis=0)
        y_ref[:, pair * LANES:(pair + 1) * LANES] = two.T.astype(BF16)


def _dsa(iq, kk, dq, vt, iwt):
    bsz, seq, _ = dq.shape
    nq = seq // DSA_TQ
    qblk = lambda w: pl.BlockSpec((None, DSA_TQ, w), lambda b, i: (b, i, 0))
    return pl.pallas_call(
        _dsa_kernel,
        grid=(bsz, nq),
        in_specs=[qblk(LANES),
                  pl.BlockSpec((None, seq, LANES), lambda b, i: (b, 0, 0)),
                  qblk(DSA_W),
                  pl.BlockSpec((None, HEAD_DIM, seq), lambda b, i: (b, 0, 0)),
                  pl.BlockSpec((None, 16, DSA_TQ), lambda b, i: (b, 0, i))],
        out_specs=qblk(DSA_W),
        out_shape=jax.ShapeDtypeStruct((bsz, seq, DSA_W), BF16),
        scratch_shapes=[pltpu.VMEM((seq, DSA_TQ), F32),
                        pltpu.VMEM((HEAD_DIM, DSA_HEADS * DSA_TQ), F32),
                        pltpu.VMEM((DSA_HEADS * DSA_TQ, HEAD_DIM), BF16),
                        pltpu.VMEM((IDX_HEADS * DSA_TQ, IDX_DIM), BF16),
                        pltpu.VMEM((seq, DSA_HEADS * DSA_TQ), F32),
                        pltpu.VMEM((8, DSA_TQ), F32),
                        pltpu.VMEM((8, DSA_TQ), jnp.int32)],
        name="dsa_group",
        compiler_params=_cparams("parallel", "parallel"),
    )(iq, kk, dq, vt, iwt)


def _moba_kernel(mq_ref, mk_ref, mvt_ref, kmean_ref, y_ref, bias_ref, o_ref, sa_ref, sb_ref):
    own = pl.program_id(1)
    blk = MOBA_BLOCK
    nb = kmean_ref.shape[0]
    heads = range(MOBA_HEADS)
    cols = [slice(h * HEAD_DIM, (h + 1) * HEAD_DIM) for h in heads]
    lanes = [slice(h * blk, (h + 1) * blk) for h in heads]
    wide = MOBA_HEADS * blk

    half_rows = blk // 2

    def scores(j, half):
        base = pl.multiple_of(j * blk + half * half_rows, half_rows)
        return jnp.concatenate(
            [_nt(mk_ref[pl.ds(base, half_rows), cols[h]], mq_ref[:, cols[h]]) for h in heads], axis=1)

    ones_rows = jnp.ones((16, half_rows), BF16)

    def softmax_step(s, m_old, l_old, j, half):
        base = pl.multiple_of(j * blk + half * half_rows, half_rows)
        m_new = jnp.maximum(m_old, jnp.max(s, axis=0, keepdims=True))
        alpha = jnp.exp2(m_old - m_new)
        pb = jnp.exp2((s - m_new).astype(BF16))
        l_new = alpha * l_old + jnp.dot(ones_rows, pb, preferred_element_type=F32)[0:1, :]
        pv = jnp.concatenate(
            [jnp.dot(mvt_ref[cols[h], pl.ds(base, half_rows)], pb[:, lanes[h]], preferred_element_type=F32)
             for h in heads], axis=1)
        o_ref[...] = alpha * o_ref[...] + pv
        return m_new, l_new

    km = kmean_ref[...]
    k_hi = km.astype(BF16).astype(F32)
    k_mid = (km - k_hi).astype(BF16).astype(F32)
    k_lo = (km - k_hi - k_mid).astype(BF16).astype(F32)
    pieces = jnp.concatenate([k_hi, k_mid, k_lo, jnp.zeros_like(km)], axis=0).astype(BF16)
    g3 = jnp.concatenate([_nt(pieces[:, cols[h]], mq_ref[:, cols[h]]) for h in heads], axis=1)
    gt = g3[0:nb] + g3[nb:2 * nb] + g3[2 * nb:3 * nb]
    ridx = lax.broadcasted_iota(jnp.int32, (nb, wide), 0)
    for j in range(nb):
        gj = gt[j:j + 1, :]
        beats = jnp.where(gt > gj, 1.0, jnp.where(gt == gj, jnp.where(ridx < j, 1.0, 0.0), 0.0))
        beats = jnp.where(ridx < own, beats, 0.0)
        rank = jnp.sum(beats, axis=0, keepdims=True)
        bias_ref[j:j + 1, :] = jnp.where(rank < float(MOBA_TOPK), 0.0, NEG)

    o_ref[...] = jnp.zeros(o_ref.shape, F32)
    carry = (jnp.full((1, wide), NEG, F32), jnp.zeros((1, wide), F32))
    for half in range(2):
        krow = lax.broadcasted_iota(jnp.int32, (half_rows, blk), 0) + half * half_rows
        causal = jnp.where(krow <= lax.broadcasted_iota(jnp.int32, (half_rows, blk), 1), 0.0, NEG)
        s = scores(own, half) + jnp.concatenate([causal] * MOBA_HEADS, axis=1)
        carry = softmax_step(s, *carry, own, half)

    sa_ref[...] = scores(0, 0) + bias_ref[0:1, :]

    def past(j, carry):
        m, l = carry
        sb_ref[...] = scores(j, 1) + bias_ref[pl.ds(j, 1), :]
        m, l = softmax_step(sa_ref[...], m, l, j, 0)
        nxt = jnp.minimum(j + 1, jnp.maximum(own - 1, 0))
        sa_ref[...] = scores(nxt, 0) + bias_ref[pl.ds(nxt, 1), :]
        return softmax_step(sb_ref[...], m, l, j, 1)

    _, l_fin = lax.fori_loop(0, own, past, carry)

    out = o_ref[...] / l_fin
    for pair in range(MOBA_HEADS // 2):
        two = jnp.concatenate([out[:, lanes[2 * pair]], out[:, lanes[2 * pair + 1]]], axis=0)
        y_ref[:, pair * LANES:(pair + 1) * LANES] = two.T.astype(BF16)


def _moba(mq, mk, mvt, kmean):
    bsz, seq, _ = mq.shape
    nb = seq // MOBA_BLOCK
    qblk = pl.BlockSpec((None, MOBA_BLOCK, MOBA_W), lambda b, i: (b, i, 0))
    return pl.pallas_call(
        _moba_kernel,
        grid=(bsz, nb),
        in_specs=[qblk,
                  pl.BlockSpec((None, seq, MOBA_W), lambda b, i: (b, 0, 0)),
                  pl.BlockSpec((None, MOBA_W, seq), lambda b, i: (b, 0, 0)),
                  pl.BlockSpec((None, nb, MOBA_W), lambda b, i: (b, 0, 0))],
        out_specs=qblk,
        out_shape=jax.ShapeDtypeStruct((bsz, seq, MOBA_W), BF16),
        scratch_shapes=[pltpu.VMEM((nb, MOBA_HEADS * MOBA_BLOCK), F32),
                        pltpu.VMEM((HEAD_DIM, MOBA_HEADS * MOBA_BLOCK), F32),
                        pltpu.VMEM((MOBA_BLOCK // 2, MOBA_HEADS * MOBA_BLOCK), F32),
                        pltpu.VMEM((MOBA_BLOCK // 2, MOBA_HEADS * MOBA_BLOCK), F32)],
        name="moba_group",
        compiler_params=_cparams("parallel", "parallel"),
    )(mq, mk, mvt, kmean)


def _pair_max(vals):
    best = None
    for a in range(len(vals)):
        for b in range(a + 1, len(vals)):
            s = vals[a] + vals[b]
            best = s if best is None else jnp.maximum(best, s)
    return best


def _tail_kernel(x_ref, yc_ref, yd_ref, ym_ref, mod_ref, gffn_ref, wo_ref, wrt_ref, br_ref,
                 wg_ref, wu_ref, wd_ref, gfin_ref, o_ref, gates_ref, *, final_norm):
    tm = x_ref.shape[0]
    mix = (jnp.dot(yc_ref[...], wo_ref[0:CONV_CH, :], preferred_element_type=F32)
           + jnp.dot(yd_ref[...], wo_ref[CONV_CH:CONV_CH + DSA_W, :], preferred_element_type=F32)
           + jnp.dot(ym_ref[...], wo_ref[CONV_CH + DSA_W:MIX_W, :], preferred_element_type=F32))
    x1 = x_ref[...] + mod_ref[2:3, :] * mix
    h = _rms_mod(x1, gffn_ref[...], mod_ref[3:4, :], mod_ref[4:5, :])
    hb = h.astype(BF16)

    aff_t = jax.nn.sigmoid(_nt(wrt_ref[...], h, precision=lax.Precision.HIGHEST))
    bia_t = aff_t + br_ref[...]
    aff = [aff_t[e:e + 1, :] for e in range(N_EXPERTS)]
    bia = [bia_t[e:e + 1, :] for e in range(N_EXPERTS)]
    gscore = [_pair_max(bia[g * EXPERTS_PER_GROUP:(g + 1) * EXPERTS_PER_GROUP]) for g in range(N_EXPERT_GROUPS)]
    best = jnp.zeros((1, tm), jnp.int32)
    best_s = gscore[0]
    for g in range(1, N_EXPERT_GROUPS):
        better = gscore[g] > best_s
        best = jnp.where(better, g, best)
        best_s = jnp.where(better, gscore[g], best_s)
    wsel = []
    for e in range(N_EXPERTS):
        g = e // EXPERTS_PER_GROUP
        rank = jnp.zeros((1, tm), F32)
        for e2 in range(g * EXPERTS_PER_GROUP, (g + 1) * EXPERTS_PER_GROUP):
            if e2 < e:
                rank = rank + jnp.where(bia[e2] >= bia[e], 1.0, 0.0)
            elif e2 > e:
                rank = rank + jnp.where(bia[e2] > bia[e], 1.0, 0.0)
        wsel.append(jnp.where(best == g, jnp.where(rank < float(TOP_K), aff[e], 0.0), 0.0))
    wsum = wsel[0]
    for e in range(1, N_EXPERTS):
        wsum = wsum + wsel[e]
    gates_t = jnp.concatenate([w / wsum for w in wsel]
                              + [jnp.zeros((LANES - N_EXPERTS, tm), F32)], axis=0)
    gates_ref[...] = gates_t.T

    acc = jnp.zeros((tm, D_MODEL), F32)
    for e in range(N_EXPERTS):
        ge = jnp.dot(hb, wg_ref[e], preferred_element_type=F32)
        ue = jnp.dot(hb, wu_ref[e], preferred_element_type=F32)
        a = ge * jax.nn.sigmoid(ge) * ue
        a = (a * gates_ref[:, e:e + 1]).astype(BF16)
        acc = acc + jnp.dot(a, wd_ref[e], preferred_element_type=F32)

    x2 = x1 + mod_ref[5:6, :] * acc
    if final_norm:
        ms = jnp.mean(x2 * x2, axis=-1, keepdims=True)
        x2 = x2 * lax.rsqrt(ms + EPS) * gfin_ref[...]
    o_ref[...] = x2


def _tail(x2d, yc, yd, ym, mod3, g_ffn, w_o, w_rt, b_r, wg, wu, wd, g_fin, seq, final_norm, tm=512):
    ntok, d = x2d.shape
    per_b = seq // tm
    tok = lambda w: pl.BlockSpec((tm, w), lambda t: (t, 0))
    const = lambda shape: pl.BlockSpec(shape, lambda t: (0,) * len(shape), pipeline_mode=pl.Buffered(1))
    return pl.pallas_call(
        functools.partial(_tail_kernel, final_norm=final_norm),
        grid=(ntok // tm,),
        in_specs=[tok(d), tok(CONV_CH), tok(DSA_W), tok(MOBA_W),
                  pl.BlockSpec((None, 6, d), lambda t: (t // per_b, 0, 0)),
                  const((1, d)),
                  const((MIX_W, d)),
                  const((N_EXPERTS, d)),
                  const((N_EXPERTS, 1)),
                  const((N_EXPERTS, d, D_EXPERT)),
                  const((N_EXPERTS, d, D_EXPERT)),
                  const((N_EXPERTS, D_EXPERT, d)),
                  const((1, d))],
        out_specs=tok(d),
        out_shape=jax.ShapeDtypeStruct((ntok, d), F32),
        scratch_shapes=[pltpu.VMEM((tm, LANES), F32)],
        name="tail_moe",
        compiler_params=_cparams("parallel"),
    )(x2d, yc, yd, ym, mod3, g_ffn, w_o, w_rt, b_r, wg, wu, wd, g_fin)


def _rope_tables(seq):
    pos = jnp.arange(seq).astype(F32)[:, None]

    def inv_freq(d):
        rot = d // ROPE_FRACTION
        return jnp.exp(-math.log(ROPE_THETA) * (jnp.arange(rot // 2, dtype=F32) * 2.0 / rot))

    inv_all = jnp.concatenate([inv_freq(HEAD_DIM), inv_freq(IDX_DIM), jnp.zeros((1,), F32)])
    n64 = HEAD_DIM // ROPE_FRACTION // 2
    n32 = IDX_DIM // ROPE_FRACTION // 2
    none = n64 + n32

    def lane_plan(widths):
        idx, first, second, width = [], [], [], []
        for d in widths:
            half = d // ROPE_FRACTION // 2
            off = 0 if d == HEAD_DIM else n64
            for i in range(d):
                idx.append(off + i % half if i < 2 * half else none)
                first.append(1.0 if i < half else 0.0)
                second.append(1.0 if half <= i < 2 * half else 0.0)
                width.append(d)
        pad = LANES - len(idx)
        return (np.array(idx + [none] * pad), np.array(first + [0.0] * pad, np.float32),
                np.array(second + [0.0] * pad, np.float32), np.array(width + [0] * pad))

    def tables(widths, split_by_width=False):
        idx, first, second, width = lane_plan(widths)
        ang = pos * inv_all[idx][None, :]
        cos, sin = jnp.cos(ang), jnp.sin(ang)
        if not split_by_width:
            return [cos, -sin * first[None, :], sin * second[None, :]]
        out = [cos]
        for d in sorted(set(widths), reverse=True):
            sel = (width == d).astype(np.float32)
            out += [-sin * (first * sel)[None, :], sin * (second * sel)[None, :]]
        return out

    return jnp.stack(tables([HEAD_DIM] * (LANES // HEAD_DIM)) + tables([IDX_DIM] * (LANES // IDX_DIM))
                     + tables([HEAD_DIM, IDX_DIM], split_by_width=True), axis=0)


def _prep_in_weights(w):
    d = w.shape[0]
    col = lambda name: w[:, _OFF[name][0]:_OFF[name][1]]
    scale = HEAD_DIM ** -0.5 * math.log2(math.e)
    w_p = jnp.concatenate([
        col("cu"), col("cg"), col("dq") * scale, col("iq"),
        col("dk"), col("ik"), jnp.zeros((d, LANES - HEAD_DIM - IDX_DIM), F32),
        col("mq") * scale, col("mk")], axis=1).astype(BF16)
    w_t = jnp.concatenate([
        col("dv"), col("mv"), col("iw"), jnp.zeros((d, N_PROJ_T - HEAD_DIM - MOBA_W - IDX_HEADS), F32)],
        axis=1).T.astype(BF16)
    return w_p, w_t


def kernel(x, c, w_ada, b_ada, g_mix, w_in, w_dw, b_dw, ln_conv_g, ln_conv_b, w_o, g_ffn, w_router, b_router, w_gate, w_up, w_down, g_final):
    bsz, seq, d = x.shape
    depth = w_ada.shape[0]
    mod = _ada(c, w_ada, b_ada).reshape(depth, bsz, 6, d)
    rope_tabs = _rope_tables(seq)
    w_rt = w_router.T
    b_r = b_router.reshape(N_EXPERTS, 1)
    g_fin = g_final.reshape(1, d)
    for l in range(depth):
        w_p, w_t = _prep_in_weights(w_in[l])
        cucg, dq, iq, kk, mq, mk, kmean, vt, mvt, iwt = _in_proj(
            x, mod[l], g_mix[l].reshape(1, d), w_p, w_t, rope_tabs)
        y_conv = _conv(cucg, w_dw[l], b_dw[l], ln_conv_g[l], ln_conv_b[l])
        y_dsa = _dsa(iq, kk, dq, vt, iwt)
        y_moba = _moba(mq, mk, mvt, kmean.reshape(bsz, seq // MOBA_BLOCK, MOBA_W))
        x = _tail(x.reshape(bsz * seq, d), y_conv.reshape(bsz * seq, CONV_CH),
                  y_dsa.reshape(bsz * seq, DSA_W), y_moba.reshape(bsz * seq, MOBA_W),
                  mod[l], g_ffn[l].reshape(1, d), w_o[l].astype(BF16), w_rt, b_r,
                  w_gate[l].astype(BF16), w_up[l].astype(BF16), w_down[l].astype(BF16), g_fin,
                  seq, final_norm=(l == depth - 1)).reshape(bsz, seq, d)
    return x
```

```python
import functools
import math

import jax
import jax.numpy as jnp
import numpy as np
from jax import lax
from jax.experimental import pallas as pl
from jax.experimental.pallas import tpu as pltpu

F32 = jnp.float32
BF16 = jnp.bfloat16

D_MODEL = 1024
HEAD_DIM = 64
CONV_CH = 256
CONV_WIDTH = 31
DSA_HEADS = 6
MOBA_HEADS = 6
DSA_W = DSA_HEADS * HEAD_DIM
MOBA_W = MOBA_HEADS * HEAD_DIM
MIX_W = CONV_CH + DSA_W + MOBA_W
IDX_HEADS = 4
IDX_DIM = 32
DSA_TOPK = 256
MOBA_BLOCK = 256
MOBA_TOPK = 3
ROPE_THETA = 500000.0
ROPE_FRACTION = 4
N_EXPERTS = 16
N_EXPERT_GROUPS = 4
EXPERTS_PER_GROUP = N_EXPERTS // N_EXPERT_GROUPS
TOP_K = 2
D_EXPERT = 256
EPS = 1e-6

LANES = 128
NEG = -1e30
VMEM_LIMIT = 56 * 1024 * 1024

_OFF = {}
_o = 0
for _name, _w in (("cu", CONV_CH), ("cg", CONV_CH), ("dq", DSA_W), ("dk", HEAD_DIM), ("dv", HEAD_DIM),
                  ("iq", IDX_HEADS * IDX_DIM), ("ik", IDX_DIM), ("iw", IDX_HEADS),
                  ("mq", MOBA_W), ("mk", MOBA_W), ("mv", MOBA_W)):
    _OFF[_name] = (_o, _o + _w)
    _o += _w

C_CUCG = (0, 512)
C_DQ = (512, 896)
C_IQ = (896, 1024)
C_KK = (1024, 1152)
C_MQ = (1152, 1536)
C_MK = (1536, 1920)
N_PROJ = 1920
N_PROJ_T = 464

NT_DIMS = (((1,), (1,)), ((), ()))


def _nt(a, b, precision=None):
    return lax.dot_general(a, b, NT_DIMS, preferred_element_type=F32, precision=precision)


def _cparams(*sem):
    return pltpu.CompilerParams(dimension_semantics=sem, vmem_limit_bytes=VMEM_LIMIT)


def _ada_kernel(c_ref, w_ref, b_ref, o_ref):
    c = c_ref[...]
    ca = c * jax.nn.sigmoid(c)
    o_ref[...] = jnp.dot(ca, w_ref[...], preferred_element_type=F32,
                         precision=lax.Precision.HIGHEST) + b_ref[...]


def _ada(c, w_ada, b_ada):
    depth, d, n = w_ada.shape
    bsz = c.shape[0]
    tn = 1536
    return pl.pallas_call(
        _ada_kernel,
        grid=(depth, n // tn),
        in_specs=[pl.BlockSpec((bsz, d), lambda l, j: (0, 0)),
                  pl.BlockSpec((None, d, tn), lambda l, j: (l, 0, j)),
                  pl.BlockSpec((None, 1, tn), lambda l, j: (l, 0, j))],
        out_specs=pl.BlockSpec((None, bsz, tn), lambda l, j: (l, 0, j)),
        out_shape=jax.ShapeDtypeStruct((depth, bsz, n), F32),
        name="ada_params",
        compiler_params=_cparams("arbitrary", "arbitrary"),
    )(c, w_ada, b_ada.reshape(depth, 1, n))


def _rms_mod(x, g, shift, scale):
    ms = jnp.mean(x * x, axis=-1, keepdims=True)
    h = x * lax.rsqrt(ms + EPS) * g
    return h * (1.0 + scale) + shift


def _in_kernel(x_ref, mod_ref, g_ref, w_ref, wt_ref, rope_ref,
               cucg_ref, dq_ref, iq_ref, kk_ref, mq_ref, mk_ref, kmean_ref, vt_ref, mvt_ref, iwt_ref):
    tm = x_ref.shape[0]
    h = _rms_mod(x_ref[...], g_ref[...], mod_ref[0:1, :], mod_ref[1:2, :])
    hb = h.astype(BF16)

    def proj(cols):
        return jnp.dot(hb, w_ref[:, cols[0]:cols[1]], preferred_element_type=F32)

    def rope(v, t0, half):
        return (v * rope_ref[t0] + pltpu.roll(v, LANES - half, 1) * rope_ref[t0 + 1]
                + pltpu.roll(v, half, 1) * rope_ref[t0 + 2])

    cucg_ref[...] = proj(C_CUCG).astype(BF16)

    p = proj(C_DQ)
    for ch in range(3):
        sl = slice(ch * LANES, (ch + 1) * LANES)
        dq_ref[:, sl] = rope(p[:, sl], 0, 8).astype(BF16)

    iq_ref[...] = rope(proj(C_IQ), 3, 4).astype(BF16)

    p = proj(C_KK)
    kk = (p * rope_ref[6] + pltpu.roll(p, LANES - 8, 1) * rope_ref[7] + pltpu.roll(p, 8, 1) * rope_ref[8]
          + pltpu.roll(p, LANES - 4, 1) * rope_ref[9] + pltpu.roll(p, 4, 1) * rope_ref[10])
    kk_ref[...] = kk.astype(BF16)

    p = proj(C_MQ)
    for ch in range(3):
        sl = slice(ch * LANES, (ch + 1) * LANES)
        mq_ref[:, sl] = rope(p[:, sl], 0, 8).astype(BF16)

    p = proj(C_MK)
    for ch in range(3):
        sl = slice(ch * LANES, (ch + 1) * LANES)
        r = rope(p[:, sl], 0, 8)
        mk_ref[:, sl] = r.astype(BF16)
        for blk in range(tm // MOBA_BLOCK):
            kmean_ref[blk:blk + 1, sl] = jnp.mean(r[blk * MOBA_BLOCK:(blk + 1) * MOBA_BLOCK], axis=0, keepdims=True)

    pt = _nt(wt_ref[...], hb)
    vt_ref[...] = pt[0:HEAD_DIM].astype(BF16)
    mvt_ref[...] = pt[HEAD_DIM:HEAD_DIM + MOBA_W].astype(BF16)
    iwt_ref[...] = pt[HEAD_DIM + MOBA_W:N_PROJ_T]


def _in_proj(x, mod3, g, w_p, w_t, rope_tabs, tm=512):
    bsz, seq, d = x.shape
    nt = seq // tm
    nb = tm // MOBA_BLOCK
    tok = lambda w: pl.BlockSpec((None, tm, w), lambda b, t: (b, t, 0))
    out_shape = (
        jax.ShapeDtypeStruct((bsz, seq, 512), BF16),
        jax.ShapeDtypeStruct((bsz, seq, DSA_W), BF16),
        jax.ShapeDtypeStruct((bsz, seq, LANES), BF16),
        jax.ShapeDtypeStruct((bsz, seq, LANES), BF16),
        jax.ShapeDtypeStruct((bsz, seq, MOBA_W), BF16),
        jax.ShapeDtypeStruct((bsz, seq, MOBA_W), BF16),
        jax.ShapeDtypeStruct((bsz, nt, nb, MOBA_W), F32),
        jax.ShapeDtypeStruct((bsz, HEAD_DIM, seq), BF16),
        jax.ShapeDtypeStruct((bsz, MOBA_W, seq), BF16),
        jax.ShapeDtypeStruct((bsz, 16, seq), F32),
    )
    out_specs = (
        tok(512), tok(DSA_W), tok(LANES), tok(LANES), tok(MOBA_W), tok(MOBA_W),
        pl.BlockSpec((None, None, nb, MOBA_W), lambda b, t: (b, t, 0, 0)),
        pl.BlockSpec((None, HEAD_DIM, tm), lambda b, t: (b, 0, t)),
        pl.BlockSpec((None, MOBA_W, tm), lambda b, t: (b, 0, t)),
        pl.BlockSpec((None, 16, tm), lambda b, t: (b, 0, t)),
    )
    return pl.pallas_call(
        _in_kernel,
        grid=(bsz, nt),
        in_specs=[tok(d),
                  pl.BlockSpec((None, 6, d), lambda b, t: (b, 0, 0)),
                  pl.BlockSpec((1, d), lambda b, t: (0, 0)),
                  pl.BlockSpec((d, N_PROJ), lambda b, t: (0, 0)),
                  pl.BlockSpec((N_PROJ_T, d), lambda b, t: (0, 0)),
                  pl.BlockSpec((11, tm, LANES), lambda b, t: (0, t, 0))],
        out_specs=out_specs,
        out_shape=out_shape,
        name="in_proj",
        compiler_params=_cparams("parallel", "parallel"),
    )(x, mod3, g, w_p, w_t, rope_tabs)


SUBLANES = 8
CONV_PAD = 32
CONV_TAIL = 16
CONV_ROWS = 128
CONV_FIRST = CONV_PAD - (CONV_WIDTH - 1)
CONV_SPAN = CONV_ROWS + 2 * SUBLANES


def _conv_kernel(cucg_ref, wdw_ref, bdw_ref, lg_ref, lb_ref, y_ref, a_ref):
    seq = cucg_ref.shape[0]
    a_ref[0:CONV_PAD, :] = jnp.zeros((CONV_PAD, CONV_CH), F32)
    a_ref[seq + CONV_PAD:seq + CONV_PAD + CONV_TAIL, :] = jnp.zeros((CONV_TAIL, CONV_CH), F32)

    def glu(c, carry):
        base = pl.multiple_of(c * 256, 256)
        u = cucg_ref[pl.ds(base, 256), 0:CONV_CH].astype(F32)
        g = cucg_ref[pl.ds(base, 256), CONV_CH:2 * CONV_CH].astype(F32)
        a_ref[pl.ds(base + CONV_PAD, 256), :] = u * jax.nn.sigmoid(g)
        return carry

    lax.fori_loop(0, seq // 256, glu, 0)

    def step(c, carry):
        base = pl.multiple_of(c * CONV_ROWS, CONV_ROWS)
        acc = jnp.zeros((CONV_ROWS, CONV_CH), F32) + bdw_ref[...]
        for r in range(SUBLANES):
            part = None
            for j in range(r, CONV_WIDTH, SUBLANES):
                term = a_ref[pl.ds(base + (j - r), CONV_SPAN), :] * wdw_ref[j:j + 1, :]
                part = term if part is None else part + term
            shift = CONV_FIRST + r
            acc = acc + part[shift:shift + CONV_ROWS, :]
        mu = jnp.mean(acc, axis=-1, keepdims=True)
        yc = acc - mu
        var = jnp.mean(yc * yc, axis=-1, keepdims=True)
        yn = yc * lax.rsqrt(var + EPS) * lg_ref[...] + lb_ref[...]
        y_ref[pl.ds(base, CONV_ROWS), :] = (yn * jax.nn.sigmoid(yn)).astype(BF16)
        return carry

    lax.fori_loop(0, seq // CONV_ROWS, step, 0)


def _conv(cucg, w_dw, b_dw, ln_g, ln_b):
    bsz, seq, _ = cucg.shape
    row = pl.BlockSpec((1, CONV_CH), lambda b: (0, 0))
    return pl.pallas_call(
        _conv_kernel,
        grid=(bsz,),
        in_specs=[pl.BlockSpec((None, seq, 2 * CONV_CH), lambda b: (b, 0, 0)),
                  pl.BlockSpec((CONV_WIDTH, CONV_CH), lambda b: (0, 0)),
                  row, row, row],
        out_specs=pl.BlockSpec((None, seq, CONV_CH), lambda b: (b, 0, 0)),
        out_shape=jax.ShapeDtypeStruct((bsz, seq, CONV_CH), BF16),
        scratch_shapes=[pltpu.VMEM((seq + CONV_PAD + CONV_TAIL, CONV_CH), F32)],
        name="conv_group",
        compiler_params=_cparams("parallel"),
    )(cucg, w_dw, b_dw.reshape(1, -1), ln_g.reshape(1, -1), ln_b.reshape(1, -1))


COUNT_BASE = 256
DSA_KC = 256
DSA_TQ = 256
INT_MIN = -2 ** 31


def _key_to_float(k):
    b = jnp.where(k < 0, jnp.int32(INT_MIN) - k, k)
    return lax.bitcast_convert_type(b, F32)


def _dsa_kernel(iq_ref, kk_ref, dq_ref, vt_ref, iwt_ref, y_ref, sc_ref, acc_ref, q_ref, iqw_ref, qk_ref, thr_ref, last_ref):
    i = pl.program_id(1)
    nch = ((i + 1) * DSA_TQ + DSA_KC - 1) // DSA_KC
    qpos = i * DSA_TQ + lax.broadcasted_iota(jnp.int32, (1, DSA_TQ), 1)
    krow = lax.broadcasted_iota(jnp.int32, (DSA_KC, DSA_TQ), 0)

    for h in range(IDX_HEADS):
        iqw_ref[h * DSA_TQ:(h + 1) * DSA_TQ, :] = iq_ref[:, h * IDX_DIM:(h + 1) * IDX_DIM]
    wwide = jnp.concatenate([iwt_ref[h:h + 1, :] for h in range(IDX_HEADS)], axis=1) * (IDX_HEADS ** -0.5)

    def idx_body(c, carry):
        base = pl.multiple_of(c * DSA_KC, DSA_KC)
        lg = _nt(kk_ref[pl.ds(base, DSA_KC), HEAD_DIM:HEAD_DIM + IDX_DIM], iqw_ref[...])
        r = jnp.maximum(lg * (IDX_DIM ** -0.5), 0.0) * wwide
        sc = r[:, 0:DSA_TQ]
        for h in range(1, IDX_HEADS):
            sc = sc + r[:, h * DSA_TQ:(h + 1) * DSA_TQ]
        sc_ref[pl.ds(base, DSA_KC), :] = jnp.where(krow + base <= qpos, sc, -jnp.inf)
        return carry

    lax.fori_loop(0, nch, idx_body, 0)

    for h in range(DSA_HEADS):
        q_ref[h * DSA_TQ:(h + 1) * DSA_TQ, :] = dq_ref[:, h * HEAD_DIM:(h + 1) * HEAD_DIM]

    def qk_chunk(c):
        rows = slice(c * DSA_KC, (c + 1) * DSA_KC)
        qk_ref[rows, :] = _nt(kk_ref[rows, 0:HEAD_DIM], q_ref[...])

    topk = float(DSA_TOPK)
    idx_bits = (kk_ref.shape[0] - 1).bit_length()
    take_all_ties = jnp.full((1, DSA_TQ), 2 ** idx_bits - 1, jnp.int32)
    row8 = lax.broadcasted_iota(jnp.int32, (SUBLANES, DSA_TQ), 0)

    def select(n):
        if n * DSA_KC <= DSA_TOPK:
            for c in range(n):
                qk_chunk(c)
            return jnp.full((1, DSA_TQ), -jnp.inf, F32), take_all_ties

        def counts(preds):
            lanes_acc = 2
            accs = [[None] * lanes_acc for _ in preds]
            for r in range(n * DSA_KC // SUBLANES):
                s = sc_ref[r * SUBLANES:(r + 1) * SUBLANES, :]
                for k, pred in enumerate(preds):
                    hit = pred(s, row8 + r * SUBLANES)
                    prev = accs[k][r % lanes_acc]
                    accs[k][r % lanes_acc] = hit if prev is None else prev + hit
            return [jnp.sum(sum(a[1:], a[0]), axis=0, keepdims=True) for a in accs]

        def at_least(tf):
            return lambda s, k: jnp.where(s >= tf, 1.0, 0.0)

        qk_chunk(0)
        (cnt,) = counts([at_least(0.0)])
        cand = jnp.where(cnt >= topk, jnp.int32(0), jnp.int32(INT_MIN))

        def one_bit(step, cand):
            t = cand | (jnp.int32(1) << (30 - step))
            (cnt,) = counts([at_least(_key_to_float(t))])
            return jnp.where(cnt >= topk, t, cand)

        def two_bits(step, cand):
            hi = jnp.int32(1) << (29 - 2 * step)
            lo = jnp.int32(1) << (28 - 2 * step)
            ts = (cand | lo, cand | hi, cand | hi | lo)
            f1, f2, f3 = (_key_to_float(t) for t in ts)
            base = float(COUNT_BASE)
            accs = [None, None]
            for r in range(n * DSA_KC // SUBLANES):
                s = sc_ref[r * SUBLANES:(r + 1) * SUBLANES, :]
                hit = jnp.where(s >= f3, 1.0 + base + base * base,
                                jnp.where(s >= f2, 1.0 + base, jnp.where(s >= f1, 1.0, 0.0)))
                accs[r % 2] = hit if accs[r % 2] is None else accs[r % 2] + hit
            c1 = c2 = c3 = 0.0
            for v in accs:
                d3 = jnp.floor(v * (1.0 / (base * base)))
                rest = v - d3 * (base * base)
                d2 = jnp.floor(rest * (1.0 / base))
                c3 = c3 + d3
                c2 = c2 + d2
                c1 = c1 + (rest - d2 * base)
            c1, c2, c3 = (jnp.sum(c, axis=0, keepdims=True) for c in (c1, c2, c3))
            return jnp.where(c3 >= topk, ts[2], jnp.where(c2 >= topk, ts[1], jnp.where(c1 >= topk, ts[0], cand)))

        if n > 1:
            qk_chunk(1)
        cand = one_bit(0, cand)
        for c in range(2, n):
            qk_chunk(c)
            cand = two_bits(c - 2, cand)
        cand = lax.fori_loop(max(n - 2, 0), 15, two_bits, cand)
        thr = jnp.where(cand == jnp.int32(INT_MIN), -jnp.inf, _key_to_float(cand))

        n_above, n_tie = counts([lambda s, k: jnp.where(s > thr, 1.0, 0.0),
                                 lambda s, k: jnp.where(s == thr, 1.0, 0.0)])
        need = topk - n_above

        def tie_body(step, last):
            t = last | (jnp.int32(1) << (idx_bits - 1 - step))
            (cnt,) = counts([lambda s, k: jnp.where(s == thr, jnp.where(k < t, 1.0, 0.0), 0.0)])
            return jnp.where(cnt < need, t, last)

        def tie_search():
            return lax.fori_loop(0, idx_bits, tie_body, jnp.zeros((1, DSA_TQ), jnp.int32))

        surplus = jnp.max(jnp.where(thr > -jnp.inf, n_tie - need, 0.0))
        return thr, lax.cond(surplus > 0.0, tie_search, lambda: take_all_ties)

    for n in range(1, kk_ref.shape[0] // DSA_KC + 1):
        @pl.when(nch == n)
        def _():
            thr_n, last_n = select(n)
            thr_ref[0:1, :] = thr_n
            last_ref[0:1, :] = last_n

    thr = thr_ref[0:1, :]
    last = last_ref[0:1, :]

    acc_ref[...] = jnp.zeros(acc_ref.shape, F32)
    wide = DSA_HEADS * DSA_TQ
    half_rows = DSA_KC
    ones_rows = jnp.ones((16, half_rows), BF16)
    hrow = lax.broadcasted_iota(jnp.int32, (half_rows, DSA_TQ), 0)

    def softmax_step(base, m_old, l_old):
        s_idx = sc_ref[pl.ds(base, half_rows), :]
        kidx = hrow + base
        bias = jnp.where(s_idx > thr, 0.0,
                         jnp.where(s_idx == thr, jnp.where(kidx <= last, 0.0, NEG), NEG))
        bias = jnp.where(kidx <= qpos, bias, NEG)
        s = qk_ref[pl.ds(base, half_rows), :] + jnp.concatenate([bias] * DSA_HEADS, axis=1)
        m_new = jnp.maximum(m_old, jnp.max(s, axis=0, keepdims=True))
        alpha = jnp.exp2(m_old - m_new)
        pb = jnp.exp2((s - m_new).astype(BF16))
        l_new = alpha * l_old + jnp.dot(ones_rows, pb, preferred_element_type=F32)[0:1, :]
        vtc = vt_ref[:, pl.ds(base, half_rows)]
        pv = jnp.concatenate(
            [jnp.dot(vtc, pb[:, h * DSA_TQ:(h + 1) * DSA_TQ], preferred_element_type=F32)
             for h in range(DSA_HEADS)], axis=1)
        acc_ref[...] = alpha * acc_ref[...] + pv
        return m_new, l_new

    def att_body(c, carry):
        return softmax_step(pl.multiple_of(c * DSA_KC, DSA_KC), *carry)

    _, l_fin = lax.fori_loop(0, nch, att_body,
                             (jnp.full((1, wide), NEG, F32), jnp.zeros((1, wide), F32)))

    out = acc_ref[...] / l_fin
    for pair in range(DSA_HEADS // 2):
        two = jnp.concatenate([out[:, (2 * pair) * DSA_TQ:(2 * pair + 1) * DSA_TQ],
                               out[:, (2 * pair + 1) * DSA_TQ:(2 * pair + 2) * DSA_TQ]], axis=0)
        y_ref[:, pair * LANES:(pair + 1) * LANES] = two.T.astype(BF16)


def _dsa(iq, kk, dq, vt, iwt):
    bsz, seq, _ = dq.shape
    nq = seq // DSA_TQ
    qblk = lambda w: pl.BlockSpec((None, DSA_TQ, w), lambda b, i: (b, i, 0))
    return pl.pallas_call(
        _dsa_kernel,
        grid=(bsz, nq),
        in_specs=[qblk(LANES),
                  pl.BlockSpec((None, seq, LANES), lambda b, i: (b, 0, 0)),
                  qblk(DSA_W),
                  pl.BlockSpec((None, HEAD_DIM, seq), lambda b, i: (b, 0, 0)),
                  pl.BlockSpec((None, 16, DSA_TQ), lambda b, i: (b, 0, i))],
        out_specs=qblk(DSA_W),
        out_shape=jax.ShapeDtypeStruct((bsz, seq, DSA_W), BF16),
        scratch_shapes=[pltpu.VMEM((seq, DSA_TQ), F32),
                        pltpu.VMEM((HEAD_DIM, DSA_HEADS * DSA_TQ), F32),
                        pltpu.VMEM((DSA_HEADS * DSA_TQ, HEAD_DIM), BF16),
                        pltpu.VMEM((IDX_HEADS * DSA_TQ, IDX_DIM), BF16),
                        pltpu.VMEM((seq, DSA_HEADS * DSA_TQ), F32),
                        pltpu.VMEM((8, DSA_TQ), F32),
                        pltpu.VMEM((8, DSA_TQ), jnp.int32)],
        name="dsa_group",
        compiler_params=_cparams("parallel", "parallel"),
    )(iq, kk, dq, vt, iwt)


def _moba_kernel(mq_ref, mk_ref, mvt_ref, kmean_ref, y_ref, bias_ref, o_ref, sa_ref, sb_ref):
    own = pl.program_id(1)
    blk = MOBA_BLOCK
    nb = kmean_ref.shape[0]
    heads = range(MOBA_HEADS)
    cols = [slice(h * HEAD_DIM, (h + 1) * HEAD_DIM) for h in heads]
    lanes = [slice(h * blk, (h + 1) * blk) for h in heads]
    wide = MOBA_HEADS * blk

    half_rows = blk // 2

    def scores(j, half):
        base = pl.multiple_of(j * blk + half * half_rows, half_rows)
        return jnp.concatenate(
            [_nt(mk_ref[pl.ds(base, half_rows), cols[h]], mq_ref[:, cols[h]]) for h in heads], axis=1)

    ones_rows = jnp.ones((16, half_rows), BF16)

    def softmax_step(s, m_old, l_old, j, half):
        base = pl.multiple_of(j * blk + half * half_rows, half_rows)
        m_new = jnp.maximum(m_old, jnp.max(s, axis=0, keepdims=True))
        alpha = jnp.exp2(m_old - m_new)
        pb = jnp.exp2((s - m_new).astype(BF16))
        l_new = alpha * l_old + jnp.dot(ones_rows, pb, preferred_element_type=F32)[0:1, :]
        pv = jnp.concatenate(
            [jnp.dot(mvt_ref[cols[h], pl.ds(base, half_rows)], pb[:, lanes[h]], preferred_element_type=F32)
             for h in heads], axis=1)
        o_ref[...] = alpha * o_ref[...] + pv
        return m_new, l_new

    km = kmean_ref[...]
    k_hi = km.astype(BF16).astype(F32)
    k_mid = (km - k_hi).astype(BF16).astype(F32)
    k_lo = (km - k_hi - k_mid).astype(BF16).astype(F32)
    pieces = jnp.concatenate([k_hi, k_mid, k_lo, jnp.zeros_like(km)], axis=0).astype(BF16)
    g3 = jnp.concatenate([_nt(pieces[:, cols[h]], mq_ref[:, cols[h]]) for h in heads], axis=1)
    gt = g3[0:nb] + g3[nb:2 * nb] + g3[2 * nb:3 * nb]
    ridx = lax.broadcasted_iota(jnp.int32, (nb, wide), 0)
    for j in range(nb):
        gj = gt[j:j + 1, :]
        beats = jnp.where(gt > gj, 1.0, jnp.where(gt == gj, jnp.where(ridx < j, 1.0, 0.0), 0.0))
        beats = jnp.where(ridx < own, beats, 0.0)
        rank = jnp.sum(beats, axis=0, keepdims=True)
        bias_ref[j:j + 1, :] = jnp.where(rank < float(MOBA_TOPK), 0.0, NEG)

    o_ref[...] = jnp.zeros(o_ref.shape, F32)
    carry = (jnp.full((1, wide), NEG, F32), jnp.zeros((1, wide), F32))
    for half in range(2):
        krow = lax.broadcasted_iota(jnp.int32, (half_rows, blk), 0) + half * half_rows
        causal = jnp.where(krow <= lax.broadcasted_iota(jnp.int32, (half_rows, blk), 1), 0.0, NEG)
        s = scores(own, half) + jnp.concatenate([causal] * MOBA_HEADS, axis=1)
        carry = softmax_step(s, *carry, own, half)

    sa_ref[...] = scores(0, 0) + bias_ref[0:1, :]

    def past(j, carry):
        m, l = carry
        sb_ref[...] = scores(j, 1) + bias_ref[pl.ds(j, 1), :]
        m, l = softmax_step(sa_ref[...], m, l, j, 0)
        nxt = jnp.minimum(j + 1, jnp.maximum(own - 1, 0))
        sa_ref[...] = scores(nxt, 0) + bias_ref[pl.ds(nxt, 1), :]
        return softmax_step(sb_ref[...], m, l, j, 1)

    _, l_fin = lax.fori_loop(0, own, past, carry)

    out = o_ref[...] / l_fin
    for pair in range(MOBA_HEADS // 2):
        two = jnp.concatenate([out[:, lanes[2 * pair]], out[:, lanes[2 * pair + 1]]], axis=0)
        y_ref[:, pair * LANES:(pair + 1) * LANES] = two.T.astype(BF16)


def _moba(mq, mk, mvt, kmean):
    bsz, seq, _ = mq.shape
    nb = seq // MOBA_BLOCK
    qblk = pl.BlockSpec((None, MOBA_BLOCK, MOBA_W), lambda b, i: (b, i, 0))
    return pl.pallas_call(
        _moba_kernel,
        grid=(bsz, nb),
        in_specs=[qblk,
                  pl.BlockSpec((None, seq, MOBA_W), lambda b, i: (b, 0, 0)),
                  pl.BlockSpec((None, MOBA_W, seq), lambda b, i: (b, 0, 0)),
                  pl.BlockSpec((None, nb, MOBA_W), lambda b, i: (b, 0, 0))],
        out_specs=qblk,
        out_shape=jax.ShapeDtypeStruct((bsz, seq, MOBA_W), BF16),
        scratch_shapes=[pltpu.VMEM((nb, MOBA_HEADS * MOBA_BLOCK), F32),
                        pltpu.VMEM((HEAD_DIM, MOBA_HEADS * MOBA_BLOCK), F32),
                        pltpu.VMEM((MOBA_BLOCK // 2, MOBA_HEADS * MOBA_BLOCK), F32),
                        pltpu.VMEM((MOBA_BLOCK // 2, MOBA_HEADS * MOBA_BLOCK), F32)],
        name="moba_group",
        compiler_params=_cparams("parallel", "parallel"),
    )(mq, mk, mvt, kmean)


def _pair_max(vals):
    best = None
    for a in range(len(vals)):
        for b in range(a + 1, len(vals)):
            s = vals[a] + vals[b]
            best = s if best is None else jnp.maximum(best, s)
    return best


def _tail_kernel(x_ref, yc_ref, yd_ref, ym_ref, mod_ref, gffn_ref, wo_ref, wrt_ref, br_ref,
                 wg_ref, wu_ref, wd_ref, gfin_ref, o_ref, gates_ref, *, final_norm):
    tm = x_ref.shape[0]
    mix = (jnp.dot(yc_ref[...], wo_ref[0:CONV_CH, :], preferred_element_type=F32)
           + jnp.dot(yd_ref[...], wo_ref[CONV_CH:CONV_CH + DSA_W, :], preferred_element_type=F32)
           + jnp.dot(ym_ref[...], wo_ref[CONV_CH + DSA_W:MIX_W, :], preferred_element_type=F32))
    x1 = x_ref[...] + mod_ref[2:3, :] * mix
    h = _rms_mod(x1, gffn_ref[...], mod_ref[3:4, :], mod_ref[4:5, :])
    hb = h.astype(BF16)

    aff_t = jax.nn.sigmoid(_nt(wrt_ref[...], h, precision=lax.Precision.HIGHEST))
    bia_t = aff_t + br_ref[...]
    aff = [aff_t[e:e + 1, :] for e in range(N_EXPERTS)]
    bia = [bia_t[e:e + 1, :] for e in range(N_EXPERTS)]
    gscore = [_pair_max(bia[g * EXPERTS_PER_GROUP:(g + 1) * EXPERTS_PER_GROUP]) for g in range(N_EXPERT_GROUPS)]
    best = jnp.zeros((1, tm), jnp.int32)
    best_s = gscore[0]
    for g in range(1, N_EXPERT_GROUPS):
        better = gscore[g] > best_s
        best = jnp.where(better, g, best)
        best_s = jnp.where(better, gscore[g], best_s)
    wsel = []
    for e in range(N_EXPERTS):
        g = e // EXPERTS_PER_GROUP
        rank = jnp.zeros((1, tm), F32)
        for e2 in range(g * EXPERTS_PER_GROUP, (g + 1) * EXPERTS_PER_GROUP):
            if e2 < e:
                rank = rank + jnp.where(bia[e2] >= bia[e], 1.0, 0.0)
            elif e2 > e:
                rank = rank + jnp.where(bia[e2] > bia[e], 1.0, 0.0)
        wsel.append(jnp.where(best == g, jnp.where(rank < float(TOP_K), aff[e], 0.0), 0.0))
    wsum = wsel[0]
    for e in range(1, N_EXPERTS):
        wsum = wsum + wsel[e]
    gates_t = jnp.concatenate([w / wsum for w in wsel]
                              + [jnp.zeros((LANES - N_EXPERTS, tm), F32)], axis=0)
    gates_ref[...] = gates_t.T

    acc = jnp.zeros((tm, D_MODEL), F32)
    for e in range(N_EXPERTS):
        ge = jnp.dot(hb, wg_ref[e], preferred_element_type=F32)
        ue = jnp.dot(hb, wu_ref[e], preferred_element_type=F32)
        a = ge * jax.nn.sigmoid(ge) * ue
        a = (a * gates_ref[:, e:e + 1]).astype(BF16)
        acc = acc + jnp.dot(a, wd_ref[e], preferred_element_type=F32)

    x2 = x1 + mod_ref[5:6, :] * acc
    if final_norm:
        ms = jnp.mean(x2 * x2, axis=-1, keepdims=True)
        x2 = x2 * lax.rsqrt(ms + EPS) * gfin_ref[...]
    o_ref[...] = x2


def _tail(x2d, yc, yd, ym, mod3, g_ffn, w_o, w_rt, b_r, wg, wu, wd, g_fin, seq, final_norm, tm=512):
    ntok, d = x2d.shape
    per_b = seq // tm
    tok = lambda w: pl.BlockSpec((tm, w), lambda t: (t, 0))
    const = lambda shape: pl.BlockSpec(shape, lambda t: (0,) * len(shape), pipeline_mode=pl.Buffered(1))
    return pl.pallas_call(
        functools.partial(_tail_kernel, final_norm=final_norm),
        grid=(ntok // tm,),
        in_specs=[tok(d), tok(CONV_CH), tok(DSA_W), tok(MOBA_W),
                  pl.BlockSpec((None, 6, d), lambda t: (t // per_b, 0, 0)),
                  const((1, d)),
                  const((MIX_W, d)),
                  const((N_EXPERTS, d)),
                  const((N_EXPERTS, 1)),
                  const((N_EXPERTS, d, D_EXPERT)),
                  const((N_EXPERTS, d, D_EXPERT)),
                  const((N_EXPERTS, D_EXPERT, d)),
                  const((1, d))],
        out_specs=tok(d),
        out_shape=jax.ShapeDtypeStruct((ntok, d), F32),
        scratch_shapes=[pltpu.VMEM((tm, LANES), F32)],
        name="tail_moe",
        compiler_params=_cparams("parallel"),
    )(x2d, yc, yd, ym, mod3, g_ffn, w_o, w_rt, b_r, wg, wu, wd, g_fin)


def _rope_tables(seq):
    pos = jnp.arange(seq).astype(F32)[:, None]

    def inv_freq(d):
        rot = d // ROPE_FRACTION
        return jnp.exp(-math.log(ROPE_THETA) * (jnp.arange(rot // 2, dtype=F32) * 2.0 / rot))

    inv_all = jnp.concatenate([inv_freq(HEAD_DIM), inv_freq(IDX_DIM), jnp.zeros((1,), F32)])
    n64 = HEAD_DIM // ROPE_FRACTION // 2
    n32 = IDX_DIM // ROPE_FRACTION // 2
    none = n64 + n32

    def lane_plan(widths):
        idx, first, second, width = [], [], [], []
        for d in widths:
            half = d // ROPE_FRACTION // 2
            off = 0 if d == HEAD_DIM else n64
            for i in range(d):
                idx.append(off + i % half if i < 2 * half else none)
                first.append(1.0 if i < half else 0.0)
                second.append(1.0 if half <= i < 2 * half else 0.0)
                width.append(d)
        pad = LANES - len(idx)
        return (np.array(idx + [none] * pad), np.array(first + [0.0] * pad, np.float32),
                np.array(second + [0.0] * pad, np.float32), np.array(width + [0] * pad))

    def tables(widths, split_by_width=False):
        idx, first, second, width = lane_plan(widths)
        ang = pos * inv_all[idx][None, :]
        cos, sin = jnp.cos(ang), jnp.sin(ang)
        if not split_by_width:
            return [cos, -sin * first[None, :], sin * second[None, :]]
        out = [cos]
        for d in sorted(set(widths), reverse=True):
            sel = (width == d).astype(np.float32)
            out += [-sin * (first * sel)[None, :], sin * (second * sel)[None, :]]
        return out

    return jnp.stack(tables([HEAD_DIM] * (LANES // HEAD_DIM)) + tables([IDX_DIM] * (LANES // IDX_DIM))
                     + tables([HEAD_DIM, IDX_DIM], split_by_width=True), axis=0)


def _prep_in_weights(w):
    d = w.shape[0]
    col = lambda name: w[:, _OFF[name][0]:_OFF[name][1]]
    scale = HEAD_DIM ** -0.5 * math.log2(math.e)
    w_p = jnp.concatenate([
        col("cu"), col("cg"), col("dq") * scale, col("iq"),
        col("dk"), col("ik"), jnp.zeros((d, LANES - HEAD_DIM - IDX_DIM), F32),
        col("mq") * scale, col("mk")], axis=1).astype(BF16)
    w_t = jnp.concatenate([
        col("dv"), col("mv"), col("iw"), jnp.zeros((d, N_PROJ_T - HEAD_DIM - MOBA_W - IDX_HEADS), F32)],
        axis=1).T.astype(BF16)
    return w_p, w_t


def kernel(x, c, w_ada, b_ada, g_mix, w_in, w_dw, b_dw, ln_conv_g, ln_conv_b, w_o, g_ffn, w_router, b_router, w_gate, w_up, w_down, g_final):
    bsz, seq, d = x.shape
    depth = w_ada.shape[0]
    mod = _ada(c, w_ada, b_ada).reshape(depth, bsz, 6, d)
    rope_tabs = _rope_tables(seq)
    w_rt = w_router.T
    b_r = b_router.reshape(N_EXPERTS, 1)
    g_fin = g_final.reshape(1, d)
    for l in range(depth):
        w_p, w_t = _prep_in_weights(w_in[l])
        cucg, dq, iq, kk, mq, mk, kmean, vt, mvt, iwt = _in_proj(
            x, mod[l], g_mix[l].reshape(1, d), w_p, w_t, rope_tabs)
        y_conv = _conv(cucg, w_dw[l], b_dw[l], ln_conv_g[l], ln_conv_b[l])
        y_dsa = _dsa(iq, kk, dq, vt, iwt)
        y_moba = _moba(mq, mk, mvt, kmean.reshape(bsz, seq // MOBA_BLOCK, MOBA_W))
        x = _tail(x.reshape(bsz * seq, d), y_conv.reshape(bsz * seq, CONV_CH),
                  y_dsa.reshape(bsz * seq, DSA_W), y_moba.reshape(bsz * seq, MOBA_W),
                  mod[l], g_ffn[l].reshape(1, d), w_o[l].astype(BF16), w_rt, b_r,
                  w_gate[l].astype(BF16), w_up[l].astype(BF16), w_down[l].astype(BF16), g_fin,
                  seq, final_norm=(l == depth - 1)).reshape(bsz, seq, d)
    return x
```
